```python
import jax, jax.numpy as jnp
from jax import lax
import numpy as np

D_MODEL = 1024
BATCH = 4
SEQ = 4096
DEPTH = 2
DEC_BATCH = 8
DEC_SEQ = 2048
PAST_LEN = 128

N_MIXERS = 2
CONV_WIDTH = 3
RET_HEADS = D_MODEL // 256
RET_QK_DIM = D_MODEL // RET_HEADS
RET_V_DIM = 2 * RET_QK_DIM
RET_QK_WIDTH = RET_HEADS * RET_QK_DIM
RET_V_WIDTH = RET_HEADS * RET_V_DIM
CHUNK = 128
D_FF = 4 * D_MODEL
NORM_EPS = 1e-6
ROPE_BASE = 10000.0

kernel_name = 'bidir_shortconv_retention_hybrid_encoder'


def rms_norm(x, g):
    xf = x.astype(jnp.float32)
    y = xf * lax.rsqrt(jnp.mean(xf * xf, axis=-1, keepdims=True) + NORM_EPS)
    return (y * g.astype(jnp.float32)).astype(x.dtype)


def short_conv_mixer(x, w_in, conv_w, conv_b, w_out):
    s = x.shape[1]
    h, gate_b, gate_c = jnp.split(x @ w_in, 3, axis=-1)
    u = gate_c * h
    pad = CONV_WIDTH // 2
    up = jnp.pad(u, ((0, 0), (pad, pad), (0, 0)))
    z = conv_b + sum(up[:, j:j + s] * conv_w[j] for j in range(CONV_WIDTH))
    return (gate_b * z) @ w_out


def rotary(x, pos):
    half = x.shape[-1] // 2
    inv = ROPE_BASE ** (-jnp.arange(half, dtype=jnp.float32) / half)
    ang = pos[:, None] * inv[None, :]
    cos, sin = jnp.cos(ang), jnp.sin(ang)
    x1, x2 = x[..., :half], x[..., half:]
    return jnp.concatenate([x1 * cos - x2 * sin, x1 * sin + x2 * cos], axis=-1)


def retention_log_decays():
    h = jnp.arange(RET_HEADS, dtype=jnp.float32)
    fwd = jnp.log(1.0 - jnp.power(2.0, -5.0 - h))
    bwd = jnp.log(1.0 - jnp.power(2.0, -5.5 - h))
    return fwd, bwd


def retention_scan(q, k, v, log_gamma, strict):
    b, h, s, dk = q.shape
    dv = v.shape[-1]
    n = s // CHUNK

    def to_chunks(t):
        return t.reshape(b, h, n, CHUNK, t.shape[-1]).transpose(2, 0, 1, 3, 4)

    qc, kc, vc = to_chunks(q), to_chunks(k), to_chunks(v)
    idx = jnp.arange(CHUNK, dtype=jnp.float32)
    diff = idx[:, None] - idx[None, :]
    keep = diff > 0 if strict else diff >= 0
    intra_decay = jnp.where(keep, jnp.exp(log_gamma[:, None, None] * jnp.maximum(diff, 0.0)), 0.0)
    q_decay = jnp.exp(log_gamma[:, None] * (idx + 1.0))[..., None]
    k_decay = jnp.exp(log_gamma[:, None] * (CHUNK - 1.0 - idx))[..., None]
    chunk_decay = jnp.exp(log_gamma * CHUNK)[:, None, None]

    def step(state, qkv):
        qi, ki, vi = qkv
        scores = jnp.einsum('bhid,bhjd->bhij', qi, ki) * intra_decay
        out = jnp.einsum('bhij,bhjv->bhiv', scores, vi)
        out = out + jnp.einsum('bhid,bhdv->bhiv', qi * q_decay, state)
        state = state * chunk_decay + jnp.einsum('bhjd,bhjv->bhdv', ki * k_decay, vi)
        return state, out

    state0 = jnp.zeros((b, h, dk, dv), jnp.float32)
    _, out = lax.scan(step, state0, (qc, kc, vc))
    return out.transpose(1, 2, 0, 3, 4).reshape(b, h, s, dv)


def retention_mixer(x, w_qkvg, w_o):
    b, s, _ = x.shape
    q, k, v, g = jnp.split(x @ w_qkvg, [RET_QK_WIDTH, 2 * RET_QK_WIDTH, 2 * RET_QK_WIDTH + RET_V_WIDTH], axis=-1)

    def heads(t, d):
        return t.reshape(b, s, RET_HEADS, d).transpose(0, 2, 1, 3).astype(jnp.float32)

    pos = jnp.arange(s, dtype=jnp.float32)
    q = rotary(heads(q, RET_QK_DIM), pos) * (RET_QK_DIM ** -0.5)
    k = rotary(heads(k, RET_QK_DIM), pos)
    v = heads(v, RET_V_DIM)
    lg_fwd, lg_bwd = retention_log_decays()
    rev = lambda t: jnp.flip(t, axis=2)
    o_fwd = retention_scan(q, k, v, lg_fwd, strict=False)
    o_bwd = rev(retention_scan(rev(q), rev(k), rev(v), lg_bwd, strict=True))
    o = o_fwd + o_bwd
    o = o - jnp.mean(o, axis=-1, keepdims=True)
    o = o * lax.rsqrt(jnp.mean(o * o, axis=-1, keepdims=True) + NORM_EPS)
    o = o.transpose(0, 2, 1, 3).reshape(b, s, RET_V_WIDTH).astype(x.dtype)
    return (jax.nn.silu(g) * o) @ w_o


def sq_relu_mlp(x, w_up, w_down):
    return jnp.square(jax.nn.relu(x @ w_up)) @ w_down


def encoder_trunk(x, norm_mix_0, w_in_conv_0, conv_w_0, conv_b_0, w_out_conv_0, norm_mlp_0, w_up_0, w_down_0,
                  norm_mix_1, w_qkvg_1, w_o_1, norm_mlp_1, w_up_1, w_down_1, norm_final):
    mixer_fns = [
        lambda t: short_conv_mixer(t, w_in_conv_0, conv_w_0, conv_b_0, w_out_conv_0),
        lambda t: retention_mixer(t, w_qkvg_1, w_o_1),
    ]
    mix_norms = [norm_mix_0, norm_mix_1]
    mlp_norms = [norm_mlp_0, norm_mlp_1]
    ups = [w_up_0, w_up_1]
    downs = [w_down_0, w_down_1]
    for i in range(DEPTH):
        x = x + mixer_fns[i % N_MIXERS](rms_norm(x, mix_norms[i]))
        x = x + sq_relu_mlp(rms_norm(x, mlp_norms[i]), ups[i], downs[i])
    return rms_norm(x, norm_final)


def setup_inputs(seed: int = 0) -> dict:
    key = jax.random.key(seed)
    ks = jax.random.split(key, 20)
    f32 = jnp.float32

    def w(k, shape, fan_in):
        return jax.random.normal(k, shape, f32) * (fan_in ** -0.5)

    def gain(k):
        return jnp.ones((D_MODEL,), f32) + 0.02 * jax.random.normal(k, (D_MODEL,), f32)

    return {
        'x_prompt': jax.random.normal(ks[0], (BATCH, SEQ, D_MODEL), f32),
        'x_sample': jax.random.normal(ks[1], (DEC_BATCH, DEC_SEQ, D_MODEL), f32),
        'norm_mix_0': gain(ks[2]),
        'w_in_conv_0': w(ks[3], (D_MODEL, 3 * D_MODEL), D_MODEL),
        'conv_w_0': w(ks[4], (CONV_WIDTH, D_MODEL), CONV_WIDTH),
        'conv_b_0': 0.02 * jax.random.normal(ks[5], (D_MODEL,), f32),
        'w_out_conv_0': w(ks[6], (D_MODEL, D_MODEL), D_MODEL),
        'norm_mlp_0': gain(ks[7]),
        'w_up_0': w(ks[8], (D_MODEL, D_FF), D_MODEL),
        'w_down_0': w(ks[9], (D_FF, D_MODEL), D_FF),
        'norm_mix_1': gain(ks[10]),
        'w_qkvg_1': w(ks[11], (D_MODEL, 2 * RET_QK_WIDTH + 2 * RET_V_WIDTH), D_MODEL),
        'w_o_1': w(ks[12], (RET_V_WIDTH, D_MODEL), RET_V_WIDTH),
        'norm_mlp_1': gain(ks[13]),
        'w_up_1': w(ks[14], (D_MODEL, D_FF), D_MODEL),
        'w_down_1': w(ks[15], (D_FF, D_MODEL), D_FF),
        'norm_final': gain(ks[16]),
    }


def reference(x_prompt, x_sample, norm_mix_0, w_in_conv_0, conv_w_0, conv_b_0, w_out_conv_0, norm_mlp_0,
              w_up_0, w_down_0, norm_mix_1, w_qkvg_1, w_o_1, norm_mlp_1, w_up_1, w_down_1, norm_final):
    y_prompt = encoder_trunk(x_prompt, norm_mix_0, w_in_conv_0, conv_w_0, conv_b_0, w_out_conv_0, norm_mlp_0,
                             w_up_0, w_down_0, norm_mix_1, w_qkvg_1, w_o_1, norm_mlp_1, w_up_1, w_down_1,
                             norm_final)
    y_sample = encoder_trunk(x_sample, norm_mix_0, w_in_conv_0, conv_w_0, conv_b_0, w_out_conv_0, norm_mlp_0,
                             w_up_0, w_down_0, norm_mix_1, w_qkvg_1, w_o_1, norm_mlp_1, w_up_1, w_down_1,
                             norm_final)
    return (y_prompt, y_sample)
```

```python
import functools

import jax
import jax.numpy as jnp
from jax import lax
from jax.experimental import pallas as pl
from jax.experimental.pallas import tpu as pltpu

D_MODEL = 1024
D_FF = 4 * D_MODEL
CONV_WIDTH = 3
RET_HEADS = 4
RET_QK_DIM = 256
RET_V_DIM = 512
RET_QK_WIDTH = RET_HEADS * RET_QK_DIM
RET_V_WIDTH = RET_HEADS * RET_V_DIM
NORM_EPS = 1e-6
ROPE_BASE = 10000.0

ROW_TILE = 512
HALO_ROWS = 8
RET_CHUNK = 256
VMEM_LIMIT_BYTES = 56 * 1024 * 1024

_F32 = jnp.float32
_BF16 = jnp.bfloat16


def _dot(a, b):
    return jnp.dot(a, b, preferred_element_type=_F32)


def _rms_norm(x, g):
    y = x * lax.rsqrt(jnp.mean(x * x, axis=-1, keepdims=True) + NORM_EPS)
    return y * g


def _resident(shape):
    zeros = (0,) * len(shape)
    return pl.BlockSpec(shape, lambda *_: zeros, pipeline_mode=pl.Buffered(1))


def _params(semantics):
    return pltpu.CompilerParams(dimension_semantics=semantics, vmem_limit_bytes=VMEM_LIMIT_BYTES)


def _conv_mixer_kernel(x_ref, xp_ref, xn_ref, g_ref, w_in_ref, cw_ref, cb_ref, w_out_ref, o_ref):
    i = pl.program_id(1)
    n = pl.num_programs(1)
    x = x_ref[0]
    g = g_ref[...]
    xn = _rms_norm(x, g).astype(_BF16)
    halo = jnp.concatenate([xp_ref[0], xn_ref[0]], axis=0)
    hn = _rms_norm(halo, g).astype(_BF16)

    d = D_MODEL
    u = _dot(xn, w_in_ref[:, 0:d]) * _dot(xn, w_in_ref[:, 2 * d:3 * d])
    uh = _dot(hn, w_in_ref[:, 0:d]) * _dot(hn, w_in_ref[:, 2 * d:3 * d])
    u_before = jnp.where(i > 0, uh[HALO_ROWS - 1:HALO_ROWS, :], 0.0)
    u_after = jnp.where(i < n - 1, uh[HALO_ROWS:HALO_ROWS + 1, :], 0.0)

    tm = x.shape[0]
    row = lax.broadcasted_iota(jnp.int32, (tm, 1), 0)
    u_prev = jnp.where(row == 0, u_before, pltpu.roll(u, 1, 0))
    u_next = jnp.where(row == tm - 1, u_after, pltpu.roll(u, tm - 1, 0))
    z = cb_ref[...] + u_prev * cw_ref[0:1, :] + u * cw_ref[1:2, :] + u_next * cw_ref[2:3, :]

    gate_b = _dot(xn, w_in_ref[:, d:2 * d])
    y = _dot((gate_b * z).astype(_BF16), w_out_ref[...])
    o_ref[0] = x + y


def _conv_mixer(x, g, w_in, conv_w, conv_b, w_out):
    b, s, d = x.shape
    tm = ROW_TILE
    nt = s // tm
    hb = tm // HALO_ROWS
    last_halo = s // HALO_ROWS - 1
    return pl.pallas_call(
        _conv_mixer_kernel,
        grid=(b, nt),
        in_specs=[
            pl.BlockSpec((1, tm, d), lambda bi, i: (bi, i, 0)),
            pl.BlockSpec((1, HALO_ROWS, d), lambda bi, i: (bi, jnp.maximum(i * hb - 1, 0), 0)),
            pl.BlockSpec((1, HALO_ROWS, d), lambda bi, i: (bi, jnp.minimum((i + 1) * hb, last_halo), 0)),
            _resident((1, d)),
            _resident((d, 3 * d)),
            _resident((CONV_WIDTH, d)),
            _resident((1, d)),
            _resident((d, d)),
        ],
        out_specs=pl.BlockSpec((1, tm, d), lambda bi, i: (bi, i, 0)),
        out_shape=jax.ShapeDtypeStruct((b, s, d), _F32),
        compiler_params=_params(("parallel", "parallel")),
        name="conv_mixer",
    )(x, x, x, g, w_in, conv_w, conv_b, w_out)


def _mlp_body(x, g_ref, w_up_ref, w_down_ref):
    xn = _rms_norm(x, g_ref[...]).astype(_BF16)
    acc = x
    for c in range(D_FF // D_MODEL):
        cols = slice(c * D_MODEL, (c + 1) * D_MODEL)
        h = jnp.square(jnp.maximum(_dot(xn, w_up_ref[:, cols]), 0.0)).astype(_BF16)
        acc = acc + _dot(h, w_down_ref[cols, :])
    return acc


def _mlp_kernel(x_ref, g_ref, w_up_ref, w_down_ref, o_ref):
    o_ref[...] = _mlp_body(x_ref[...], g_ref, w_up_ref, w_down_ref)


def _proj_mlp_norm_kernel(x_ref, a_ref, w_o_ref, g_ref, w_up_ref, w_down_ref, gf_ref, o_ref):
    x = x_ref[...] + _dot(a_ref[...], w_o_ref[...])
    y = _mlp_body(x, g_ref, w_up_ref, w_down_ref)
    o_ref[...] = _rms_norm(y, gf_ref[...])


def _mlp(x, g, w_up, w_down):
    t, d = x.shape
    tm = ROW_TILE
    rows = pl.BlockSpec((tm, d), lambda i: (i, 0))
    return pl.pallas_call(
        _mlp_kernel,
        grid=(t // tm,),
        in_specs=[rows, _resident((1, d)), _resident((d, D_FF)), _resident((D_FF, d))],
        out_specs=rows,
        out_shape=jax.ShapeDtypeStruct((t, d), _F32),
        compiler_params=_params(("parallel",)),
        name="mlp0",
    )(x, g, w_up, w_down)


def _proj_mlp_norm(x, a, w_o, g, w_up, w_down, g_final):
    t, d = x.shape
    tm = ROW_TILE
    rows = pl.BlockSpec((tm, d), lambda i: (i, 0))
    return pl.pallas_call(
        _proj_mlp_norm_kernel,
        grid=(t // tm,),
        in_specs=[rows, pl.BlockSpec((tm, RET_V_WIDTH), lambda i: (i, 0)), _resident((RET_V_WIDTH, d)),
                  _resident((1, d)), _resident((d, D_FF)), _resident((D_FF, d)), _resident((1, d))],
        out_specs=rows,
        out_shape=jax.ShapeDtypeStruct((t, d), _F32),
        compiler_params=_params(("parallel",)),
        name="proj_mlp1_norm",
    )(x, a, w_o, g, w_up, w_down, g_final)


def _rotary(t, cos, sin):
    half = RET_QK_DIM // 2
    parts = []
    for h in range(RET_HEADS):
        x1 = t[:, h * RET_QK_DIM:h * RET_QK_DIM + half]
        x2 = t[:, h * RET_QK_DIM + half:(h + 1) * RET_QK_DIM]
        parts += [x1 * cos - x2 * sin, x1 * sin + x2 * cos]
    return jnp.concatenate(parts, axis=-1)


def _qkvg_kernel(x_ref, g_ref, w_ref, cos_ref, sin_ref, q_ref, k_ref, v_ref, sg_ref):
    xn = _rms_norm(x_ref[0], g_ref[...]).astype(_BF16)
    cos, sin = cos_ref[...], sin_ref[...]
    qw, vw = RET_QK_WIDTH, RET_V_WIDTH
    q = _rotary(_dot(xn, w_ref[:, 0:qw]), cos, sin) * (RET_QK_DIM ** -0.5)
    q_ref[0] = q.astype(_BF16)
    k_ref[0] = _rotary(_dot(xn, w_ref[:, qw:2 * qw]), cos, sin).astype(_BF16)
    v_ref[0] = _dot(xn, w_ref[:, 2 * qw:2 * qw + vw]).astype(_BF16)
    gate = _dot(xn, w_ref[:, 2 * qw + vw:2 * qw + 2 * vw])
    sg_ref[0] = jax.nn.silu(gate).astype(_BF16)


def _qkvg(x, g, w, cos, sin):
    b, s, d = x.shape
    tm = ROW_TILE
    half = RET_QK_DIM // 2

    def rows(width):
        return pl.BlockSpec((1, tm, width), lambda bi, i: (bi, i, 0))

    table = pl.BlockSpec((tm, half), lambda bi, i: (i, 0))
    return pl.pallas_call(
        _qkvg_kernel,
        grid=(b, s // tm),
        in_specs=[rows(d), _resident((1, d)), _resident((d, 2 * RET_QK_WIDTH + 2 * RET_V_WIDTH)), table, table],
        out_specs=[rows(RET_QK_WIDTH), rows(RET_QK_WIDTH), rows(RET_V_WIDTH), rows(RET_V_WIDTH)],
        out_shape=[jax.ShapeDtypeStruct((b, s, RET_QK_WIDTH), _BF16),
                   jax.ShapeDtypeStruct((b, s, RET_QK_WIDTH), _BF16),
                   jax.ShapeDtypeStruct((b, s, RET_V_WIDTH), _BF16),
                   jax.ShapeDtypeStruct((b, s, RET_V_WIDTH), _BF16)],
        compiler_params=_params(("parallel", "parallel")),
        name="qkvg_rotary",
    )(x, g, w, cos, sin)


def _retention_kernel(q_ref, k_ref, v_ref, sg_ref, dmat_ref, dqf_ref, dkf_ref, cdf_ref, dqb_ref, dkb_ref, cdb_ref,
                      o_ref, ob_ref, state_ref):
    c = RET_CHUNK
    n_chunks = q_ref.shape[1] // c
    contract_rows = (((0,), (0,)), ((), ()))
    contract_cols = (((1,), (1,)), ((), ()))

    def chunk(j):
        return pl.ds(pl.multiple_of(j * c, c), c)

    state_ref[...] = jnp.zeros_like(state_ref)

    def later_chunks(t, carry):
        rows = chunk(n_chunks - 1 - t)
        state = state_ref[...]
        ob_ref[rows, :] = _dot(q_ref[0, rows, :], state.astype(_BF16)) * dqb_ref[0]
        kd = (k_ref[0, rows, :].astype(_F32) * dkb_ref[0]).astype(_BF16)
        kv = lax.dot_general(kd, v_ref[0, rows, :], contract_rows, preferred_element_type=_F32)
        state_ref[...] = state * cdb_ref[0] + kv
        return carry

    lax.fori_loop(0, n_chunks, later_chunks, 0)

    state_ref[...] = jnp.zeros_like(state_ref)

    def earlier_chunks(j, carry):
        rows = chunk(j)
        qc, kc, vc = q_ref[0, rows, :], k_ref[0, rows, :], v_ref[0, rows, :]
        state = state_ref[...]
        scores = lax.dot_general(qc, kc, contract_cols, preferred_element_type=_F32) * dmat_ref[0]
        o = _dot(scores.astype(_BF16), vc) + _dot(qc, state.astype(_BF16)) * dqf_ref[0] + ob_ref[rows, :]
        kd = (kc.astype(_F32) * dkf_ref[0]).astype(_BF16)
        kv = lax.dot_general(kd, vc, contract_rows, preferred_element_type=_F32)
        state_ref[...] = state * cdf_ref[0] + kv
        o = o - jnp.mean(o, axis=-1, keepdims=True)
        o = o * lax.rsqrt(jnp.mean(o * o, axis=-1, keepdims=True) + NORM_EPS)
        o_ref[0, rows, :] = (sg_ref[0, rows, :].astype(_F32) * o).astype(_BF16)
        return carry

    lax.fori_loop(0, n_chunks, earlier_chunks, 0)


def _retention_tables():
    c = RET_CHUNK
    h = jnp.arange(RET_HEADS, dtype=_F32)
    lg_f = jnp.log(1.0 - jnp.power(2.0, -5.0 - h))[:, None, None]
    lg_b = jnp.log(1.0 - jnp.power(2.0, -5.5 - h))[:, None, None]
    idx = jnp.arange(c, dtype=_F32)
    diff = idx[:, None] - idx[None, :]
    dmat = jnp.where(diff >= 0, jnp.exp(lg_f * jnp.maximum(diff, 0.0)), jnp.exp(lg_b * jnp.maximum(-diff, 0.0)))
    col = idx[None, :, None]

    def wide(t, width):
        return jnp.broadcast_to(t, (RET_HEADS, t.shape[1], width))

    dqf = wide(jnp.exp(lg_f * (col + 1.0)), RET_V_DIM)
    dkf = wide(jnp.exp(lg_f * (c - 1.0 - col)), RET_QK_DIM)
    cdf = wide(jnp.exp(lg_f * c), RET_V_DIM)
    dqb = wide(jnp.exp(lg_b * (c - col)), RET_V_DIM)
    dkb = wide(jnp.exp(lg_b * col), RET_QK_DIM)
    cdb = wide(jnp.exp(lg_b * c), RET_V_DIM)
    return dmat, dqf, dkf, cdf, dqb, dkb, cdb


def _retention(q, k, v, sg, tables):
    b, s, _ = q.shape
    c = RET_CHUNK

    def seq(width):
        return pl.BlockSpec((1, s, width), lambda bi, hi: (bi, 0, hi))

    def per_head(rows, width):
        return pl.BlockSpec((1, rows, width), lambda bi, hi: (hi, 0, 0))

    return pl.pallas_call(
        _retention_kernel,
        grid=(b, RET_HEADS),
        in_specs=[seq(RET_QK_DIM), seq(RET_QK_DIM), seq(RET_V_DIM), seq(RET_V_DIM),
                  per_head(c, c), per_head(c, RET_V_DIM), per_head(c, RET_QK_DIM), per_head(1, RET_V_DIM),
                  per_head(c, RET_V_DIM), per_head(c, RET_QK_DIM), per_head(1, RET_V_DIM)],
        out_specs=seq(RET_V_DIM),
        out_shape=jax.ShapeDtypeStruct((b, s, RET_V_WIDTH), _BF16),
        scratch_shapes=[pltpu.VMEM((s, RET_V_DIM), _F32), pltpu.VMEM((RET_QK_DIM, RET_V_DIM), _F32)],
        compiler_params=_params(("parallel", "arbitrary")),
        name="retention",
    )(q, k, v, sg, *tables)


def _rotary_tables(s):
    half = RET_QK_DIM // 2
    inv = ROPE_BASE ** (-jnp.arange(half, dtype=_F32) / half)
    ang = jnp.arange(s, dtype=_F32)[:, None] * inv[None, :]
    return jnp.cos(ang), jnp.sin(ang)


def _trunk(x, p, tables):
    b, s, d = x.shape
    x = _conv_mixer(x, p["norm_mix_0"], p["w_in_conv_0"], p["conv_w_0"], p["conv_b_0"], p["w_out_conv_0"])
    x = _mlp(x.reshape(b * s, d), p["norm_mlp_0"], p["w_up_0"], p["w_down_0"])
    cos, sin = _rotary_tables(s)
    q, k, v, sg = _qkvg(x.reshape(b, s, d), p["norm_mix_1"], p["w_qkvg_1"], cos, sin)
    a = _retention(q, k, v, sg, tables)
    y = _proj_mlp_norm(x, a.reshape(b * s, RET_V_WIDTH), p["w_o_1"], p["norm_mlp_1"], p["w_up_1"], p["w_down_1"],
                       p["norm_final"])
    return y.reshape(b, s, d)


def kernel(x_prompt, x_sample, norm_mix_0, w_in_conv_0, conv_w_0, conv_b_0, w_out_conv_0, norm_mlp_0, w_up_0, w_down_0,
           norm_mix_1, w_qkvg_1, w_o_1, norm_mlp_1, w_up_1, w_down_1, norm_final):
    row = lambda t: t.reshape(1, -1)
    p = {
        "norm_mix_0": row(norm_mix_0), "w_in_conv_0": w_in_conv_0.astype(_BF16), "conv_w_0": conv_w_0,
        "conv_b_0": row(conv_b_0), "w_out_conv_0": w_out_conv_0.astype(_BF16),
        "norm_mlp_0": row(norm_mlp_0), "w_up_0": w_up_0.astype(_BF16), "w_down_0": w_down_0.astype(_BF16),
        "norm_mix_1": row(norm_mix_1), "w_qkvg_1": w_qkvg_1.astype(_BF16), "w_o_1": w_o_1.astype(_BF16),
        "norm_mlp_1": row(norm_mlp_1), "w_up_1": w_up_1.astype(_BF16), "w_down_1": w_down_1.astype(_BF16),
        "norm_final": row(norm_final),
    }
    tables = _retention_tables()
    return _trunk(x_prompt, p, tables), _trunk(x_sample, p, tables)
```

```python
import jax
import jax.numpy as jnp
from jax import lax
from jax.experimental import pallas as pl
from jax.experimental.pallas import tpu as pltpu

D_MODEL = 1024
D_FF = 4 * D_MODEL
CONV_WIDTH = 3
RET_HEADS = 4
RET_QK_DIM = 256
RET_V_DIM = 512
RET_QK_WIDTH = RET_HEADS * RET_QK_DIM
RET_V_WIDTH = RET_HEADS * RET_V_DIM
NORM_EPS = 1e-6
ROPE_BASE = 10000.0

ROW_TILE = 512
MLP_ROW_TILE = 1024
HALO_ROWS = 8
RET_CHUNK = 256
VMEM_LIMIT_BYTES = 56 * 1024 * 1024

_F32 = jnp.float32
_BF16 = jnp.bfloat16


def _dot(a, b):
    return jnp.dot(a, b, preferred_element_type=_F32)


def _rms_norm(x, g):
    y = x * lax.rsqrt(jnp.mean(x * x, axis=-1, keepdims=True) + NORM_EPS)
    return y * g


def _resident(shape):
    zeros = (0,) * len(shape)
    return pl.BlockSpec(shape, lambda *_: zeros, pipeline_mode=pl.Buffered(1))


def _params(semantics):
    return pltpu.CompilerParams(dimension_semantics=semantics, vmem_limit_bytes=VMEM_LIMIT_BYTES)


def _conv_mixer_kernel(x_ref, xp_ref, xn_ref, g_ref, w_in_ref, cw_ref, cb_ref, w_out_ref, o_ref):
    i = pl.program_id(1)
    n = pl.num_programs(1)
    x = x_ref[0]
    tm = x.shape[0]
    ext = tm + 2 * HALO_ROWS
    lo, hi = HALO_ROWS, HALO_ROWS + tm
    y_ext = _rms_norm(jnp.concatenate([xp_ref[0], x, xn_ref[0]], axis=0), g_ref[...])
    xe = y_ext.astype(_BF16)
    xm = y_ext[lo:hi].astype(_BF16)

    d = D_MODEL
    u_ext = _dot(xe, w_in_ref[:, 0:d]) * _dot(xe, w_in_ref[:, 2 * d:3 * d])
    row = lax.broadcasted_iota(jnp.int32, (ext, 1), 0)
    inside = jnp.logical_and(jnp.logical_or(row >= lo, i > 0), jnp.logical_or(row < hi, i < n - 1))
    u_ext = jnp.where(inside, u_ext, 0.0)
    u_prev = pltpu.roll(u_ext, 1, 0)[lo:hi]
    u_next = pltpu.roll(u_ext, ext - 1, 0)[lo:hi]
    z = cb_ref[...] + u_prev * cw_ref[0:1, :] + u_ext[lo:hi] * cw_ref[1:2, :] + u_next * cw_ref[2:3, :]

    gate_b = _dot(xm, w_in_ref[:, d:2 * d])
    y = _dot((gate_b * z).astype(_BF16), w_out_ref[...])
    o_ref[0] = x + y


def _conv_mixer(x, g, w_in, conv_w, conv_b, w_out):
    b, s, d = x.shape
    tm = ROW_TILE
    nt = s // tm
    hb = tm // HALO_ROWS
    last_halo = s // HALO_ROWS - 1
    return pl.pallas_call(
        _conv_mixer_kernel,
        grid=(b, nt),
        in_specs=[
            pl.BlockSpec((1, tm, d), lambda bi, i: (bi, i, 0)),
            pl.BlockSpec((1, HALO_ROWS, d), lambda bi, i: (bi, jnp.maximum(i * hb - 1, 0), 0)),
            pl.BlockSpec((1, HALO_ROWS, d), lambda bi, i: (bi, jnp.minimum((i + 1) * hb, last_halo), 0)),
            _resident((1, d)),
            _resident((d, 3 * d)),
            _resident((CONV_WIDTH, d)),
            _resident((1, d)),
            _resident((d, d)),
        ],
        out_specs=pl.BlockSpec((1, tm, d), lambda bi, i: (bi, i, 0)),
        out_shape=jax.ShapeDtypeStruct((b, s, d), _F32),
        compiler_params=_params(("parallel", "parallel")),
        name="conv_mixer",
    )(x, x, x, g, w_in, conv_w, conv_b, w_out)


def _mlp_body(x, g_ref, w_up_ref, w_down_ref):
    xn = _rms_norm(x, g_ref[...]).astype(_BF16)
    acc = x
    for c in range(D_FF // D_MODEL):
        cols = slice(c * D_MODEL, (c + 1) * D_MODEL)
        h = jnp.square(jnp.maximum(_dot(xn, w_up_ref[:, cols]), 0.0)).astype(_BF16)
        acc = acc + _dot(h, w_down_ref[cols, :])
    return acc


def _mlp_kernel(x_ref, g_ref, w_up_ref, w_down_ref, o_ref):
    o_ref[...] = _mlp_body(x_ref[...], g_ref, w_up_ref, w_down_ref)


def _proj_mlp_norm_kernel(x_ref, r_ref, sg_ref, w_o_ref, g_ref, w_up_ref, w_down_ref, gf_ref, o_ref):
    gated = []
    for h in range(RET_HEADS):
        cols = slice(h * RET_V_DIM, (h + 1) * RET_V_DIM)
        o = r_ref[:, cols]
        o = o - jnp.mean(o, axis=-1, keepdims=True)
        o = o * lax.rsqrt(jnp.mean(o * o, axis=-1, keepdims=True) + NORM_EPS)
        gated.append((sg_ref[:, cols].astype(_F32) * o).astype(_BF16))
    x = x_ref[...] + _dot(jnp.concatenate(gated, axis=1), w_o_ref[...])
    y = _mlp_body(x, g_ref, w_up_ref, w_down_ref)
    o_ref[...] = _rms_norm(y, gf_ref[...])


def _mlp(x, g, w_up, w_down):
    t, d = x.shape
    tm = MLP_ROW_TILE
    rows = pl.BlockSpec((tm, d), lambda i: (i, 0))
    return pl.pallas_call(
        _mlp_kernel,
        grid=(t // tm,),
        in_specs=[rows, _resident((1, d)), _resident((d, D_FF)), _resident((D_FF, d))],
        out_specs=rows,
        out_shape=jax.ShapeDtypeStruct((t, d), _F32),
        compiler_params=_params(("parallel",)),
        name="mlp0",
    )(x, g, w_up, w_down)


def _proj_mlp_norm(x, r, sg, w_o, g, w_up, w_down, g_final):
    t, d = x.shape
    tm = ROW_TILE
    rows = pl.BlockSpec((tm, d), lambda i: (i, 0))
    wide = pl.BlockSpec((tm, RET_V_WIDTH), lambda i: (i, 0))
    return pl.pallas_call(
        _proj_mlp_norm_kernel,
        grid=(t // tm,),
        in_specs=[rows, wide, wide, _resident((RET_V_WIDTH, d)),
                  _resident((1, d)), _resident((d, D_FF)), _resident((D_FF, d)), _resident((1, d))],
        out_specs=rows,
        out_shape=jax.ShapeDtypeStruct((t, d), _F32),
        compiler_params=_params(("parallel",)),
        name="proj_mlp1_norm",
    )(x, r, sg, w_o, g, w_up, w_down, g_final)


def _rotary(t, cos, sin):
    half = RET_QK_DIM // 2
    parts = []
    for h in range(RET_HEADS):
        x1 = t[:, h * RET_QK_DIM:h * RET_QK_DIM + half]
        x2 = t[:, h * RET_QK_DIM + half:(h + 1) * RET_QK_DIM]
        parts += [x1 * cos - x2 * sin, x1 * sin + x2 * cos]
    return jnp.concatenate(parts, axis=-1)


def _qkvg_kernel(x_ref, g_ref, w_ref, cos_ref, sin_ref, q_ref, k_ref, v_ref, sg_ref):
    xn = _rms_norm(x_ref[0], g_ref[...]).astype(_BF16)
    cos, sin = cos_ref[...], sin_ref[...]
    qw, vw = RET_QK_WIDTH, RET_V_WIDTH
    gate = _dot(xn, w_ref[:, 2 * qw + vw:2 * qw + 2 * vw])
    sg_ref[0] = jax.nn.silu(gate).astype(_BF16)
    q = _rotary(_dot(xn, w_ref[:, 0:qw]), cos, sin) * (RET_QK_DIM ** -0.5)
    q_ref[0] = q.astype(_BF16)
    k_ref[0] = _rotary(_dot(xn, w_ref[:, qw:2 * qw]), cos, sin).astype(_BF16)
    v_ref[0] = _dot(xn, w_ref[:, 2 * qw:2 * qw + vw]).astype(_BF16)


def _qkvg(x, g, w, cos, sin):
    b, s, d = x.shape
    tm = ROW_TILE
    half = RET_QK_DIM // 2

    def rows(width):
        return pl.BlockSpec((1, tm, width), lambda bi, i: (bi, i, 0))

    table = pl.BlockSpec((tm, half), lambda bi, i: (i, 0))
    return pl.pallas_call(
        _qkvg_kernel,
        grid=(b, s // tm),
        in_specs=[rows(d), _resident((1, d)), _resident((d, 2 * RET_QK_WIDTH + 2 * RET_V_WIDTH)), table, table],
        out_specs=[rows(RET_QK_WIDTH), rows(RET_QK_WIDTH), rows(RET_V_WIDTH), rows(RET_V_WIDTH)],
        out_shape=[jax.ShapeDtypeStruct((b, s, RET_QK_WIDTH), _BF16),
                   jax.ShapeDtypeStruct((b, s, RET_QK_WIDTH), _BF16),
                   jax.ShapeDtypeStruct((b, s, RET_V_WIDTH), _BF16),
                   jax.ShapeDtypeStruct((b, s, RET_V_WIDTH), _BF16)],
        compiler_params=_params(("parallel", "parallel")),
        name="qkvg_rotary",
    )(x, g, w, cos, sin)


def _retention_kernel(q_ref, k_ref, v_ref, dmat_ref, dqf_ref, dkf_ref, cdf_ref, dqb_ref, dkb_ref, cdb_ref,
                      o_ref, states_ref, sf_ref, sb_ref):
    c = RET_CHUNK
    dk = RET_QK_DIM
    n_chunks = q_ref.shape[1] // c
    contract_rows = (((0,), (0,)), ((), ()))
    contract_cols = (((1,), (1,)), ((), ()))

    def chunk(j):
        return pl.ds(pl.multiple_of(j * c, c), c)

    def decayed_kv(rows, dk_ref):
        kd = (k_ref[0, rows, :].astype(_F32) * dk_ref[0]).astype(_BF16)
        return lax.dot_general(kd, v_ref[0, rows, :], contract_rows, preferred_element_type=_F32)

    sf_ref[...] = jnp.zeros_like(sf_ref)
    sb_ref[...] = jnp.zeros_like(sb_ref)
    states_ref[0, 0:dk, :] = jnp.zeros((dk, RET_V_DIM), _BF16)
    states_ref[n_chunks - 1, dk:2 * dk, :] = jnp.zeros((dk, RET_V_DIM), _BF16)

    def scan(t, carry):
        sf = sf_ref[...] * cdf_ref[0] + decayed_kv(chunk(t), dkf_ref)
        sf_ref[...] = sf
        states_ref[t + 1, 0:dk, :] = sf.astype(_BF16)
        j = n_chunks - 1 - t
        sb = sb_ref[...] * cdb_ref[0] + decayed_kv(chunk(j), dkb_ref)
        sb_ref[...] = sb
        states_ref[j - 1, dk:2 * dk, :] = sb.astype(_BF16)
        return carry

    lax.fori_loop(0, n_chunks - 1, scan, 0, unroll=True)

    def outputs(j, carry):
        rows = chunk(j)
        qc = q_ref[0, rows, :]
        scores = lax.dot_general(qc, k_ref[0, rows, :], contract_cols, preferred_element_type=_F32) * dmat_ref[0]
        qf = qc.astype(_F32)
        qq = jnp.concatenate([(qf * dqf_ref[0]).astype(_BF16), (qf * dqb_ref[0]).astype(_BF16)], axis=1)
        o_ref[0, rows, :] = _dot(scores.astype(_BF16), v_ref[0, rows, :]) + _dot(qq, states_ref[j])
        return carry

    lax.fori_loop(0, n_chunks, outputs, 0, unroll=True)


def _retention_tables():
    c = RET_CHUNK
    h = jnp.arange(RET_HEADS, dtype=_F32)
    lg_f = jnp.log(1.0 - jnp.power(2.0, -5.0 - h))[:, None, None]
    lg_b = jnp.log(1.0 - jnp.power(2.0, -5.5 - h))[:, None, None]
    idx = jnp.arange(c, dtype=_F32)
    diff = idx[:, None] - idx[None, :]
    dmat = jnp.where(diff >= 0, jnp.exp(lg_f * jnp.maximum(diff, 0.0)), jnp.exp(lg_b * jnp.maximum(-diff, 0.0)))
    col = idx[None, :, None]

    def wide(t, width):
        return jnp.broadcast_to(t, (RET_HEADS, t.shape[1], width))

    dqf = wide(jnp.exp(lg_f * (col + 1.0)), RET_QK_DIM)
    dkf = wide(jnp.exp(lg_f * (c - 1.0 - col)), RET_QK_DIM)
    cdf = wide(jnp.exp(lg_f * c), RET_V_DIM)
    dqb = wide(jnp.exp(lg_b * (c - col)), RET_QK_DIM)
    dkb = wide(jnp.exp(lg_b * col), RET_QK_DIM)
    cdb = wide(jnp.exp(lg_b * c), RET_V_DIM)
    return dmat, dqf, dkf, cdf, dqb, dkb, cdb


def _retention(q, k, v, tables):
    b, s, _ = q.shape
    c = RET_CHUNK

    def seq(width):
        return pl.BlockSpec((1, s, width), lambda bi, hi: (bi, 0, hi))

    def per_head(rows, width):
        return pl.BlockSpec((1, rows, width), lambda bi, hi: (hi, 0, 0))

    return pl.pallas_call(
        _retention_kernel,
        grid=(b, RET_HEADS),
        in_specs=[seq(RET_QK_DIM), seq(RET_QK_DIM), seq(RET_V_DIM),
                  per_head(c, c), per_head(c, RET_QK_DIM), per_head(c, RET_QK_DIM), per_head(1, RET_V_DIM),
                  per_head(c, RET_QK_DIM), per_head(c, RET_QK_DIM), per_head(1, RET_V_DIM)],
        out_specs=seq(RET_V_DIM),
        out_shape=jax.ShapeDtypeStruct((b, s, RET_V_WIDTH), _F32),
        scratch_shapes=[pltpu.VMEM((s // c, 2 * RET_QK_DIM, RET_V_DIM), _BF16),
                        pltpu.VMEM((RET_QK_DIM, RET_V_DIM), _F32), pltpu.VMEM((RET_QK_DIM, RET_V_DIM), _F32)],
        compiler_params=_params(("parallel", "arbitrary")),
        name="retention",
    )(q, k, v, *tables)


def _rotary_tables(s):
    half = RET_QK_DIM // 2
    inv = ROPE_BASE ** (-jnp.arange(half, dtype=_F32) / half)
    ang = jnp.arange(s, dtype=_F32)[:, None] * inv[None, :]
    return jnp.cos(ang), jnp.sin(ang)


def _trunk(x, p, tables):
    b, s, d = x.shape
    x = _conv_mixer(x, p["norm_mix_0"], p["w_in_conv_0"], p["conv_w_0"], p["conv_b_0"], p["w_out_conv_0"])
    x = _mlp(x.reshape(b * s, d), p["norm_mlp_0"], p["w_up_0"], p["w_down_0"])
    cos, sin = _rotary_tables(s)
    q, k, v, sg = _qkvg(x.reshape(b, s, d), p["norm_mix_1"], p["w_qkvg_1"], cos, sin)
    r = _retention(q, k, v, tables)
    y = _proj_mlp_norm(x, r.reshape(b * s, RET_V_WIDTH), sg.reshape(b * s, RET_V_WIDTH), p["w_o_1"],
                       p["norm_mlp_1"], p["w_up_1"], p["w_down_1"], p["norm_final"])
    return y.reshape(b, s, d)


def kernel(x_prompt, x_sample, norm_mix_0, w_in_conv_0, conv_w_0, conv_b_0, w_out_conv_0, norm_mlp_0, w_up_0, w_down_0,
           norm_mix_1, w_qkvg_1, w_o_1, norm_mlp_1, w_up_1, w_down_1, norm_final):
    row = lambda t: t.reshape(1, -1)
    p = {
        "norm_mix_0": row(norm_mix_0), "w_in_conv_0": w_in_conv_0.astype(_BF16), "conv_w_0": conv_w_0,
        "conv_b_0": row(conv_b_0), "w_out_conv_0": w_out_conv_0.astype(_BF16),
        "norm_mlp_0": row(norm_mlp_0), "w_up_0": w_up_0.astype(_BF16), "w_down_0": w_down_0.astype(_BF16),
        "norm_mix_1": row(norm_mix_1), "w_qkvg_1": w_qkvg_1.astype(_BF16), "w_o_1": w_o_1.astype(_BF16),
        "norm_mlp_1": row(norm_mlp_1), "w_up_1": w_up_1.astype(_BF16), "w_down_1": w_down_1.astype(_BF16),
        "norm_final": row(norm_final),
    }
    tables = _retention_tables()
    return _trunk(x_prompt, p, tables), _trunk(x_sample, p, tables)
```

```python
import functools

import jax
import jax.numpy as jnp
from jax import lax
from jax.experimental import pallas as pl
from jax.experimental.pallas import tpu as pltpu

D_MODEL = 1024
D_FF = 4 * D_MODEL
CONV_WIDTH = 3
RET_HEADS = 4
RET_QK_DIM = 256
RET_V_DIM = 512
RET_QK_WIDTH = RET_HEADS * RET_QK_DIM
RET_V_WIDTH = RET_HEADS * RET_V_DIM
NORM_EPS = 1e-6
ROPE_BASE = 10000.0

ROW_TILE = 512
MLP_ROW_TILE = 1024
HALO_ROWS = 8
RET_CHUNK = 256
VMEM_LIMIT_BYTES = 56 * 1024 * 1024

_F32 = jnp.float32
_BF16 = jnp.bfloat16


def _dot(a, b):
    return jnp.dot(a, b, preferred_element_type=_F32)


def _rms_norm(x, g):
    y = x * lax.rsqrt(jnp.mean(x * x, axis=-1, keepdims=True) + NORM_EPS)
    return y * g


def _resident(shape):
    zeros = (0,) * len(shape)
    return pl.BlockSpec(shape, lambda *_: zeros, pipeline_mode=pl.Buffered(1))


def _params(semantics):
    return pltpu.CompilerParams(dimension_semantics=semantics, vmem_limit_bytes=VMEM_LIMIT_BYTES)


def _cast_specs(weights, n_steps, step_of):
    in_specs, out_specs, out_shapes = [], [], []
    for w in weights:
        rows, cols = w.shape
        block = (rows // n_steps, cols)
        in_specs.append(pl.BlockSpec(block, lambda *idx: (step_of(*idx), 0)))
        out_specs.append(pl.BlockSpec(block, lambda *idx: (step_of(*idx), 0)))
        out_shapes.append(jax.ShapeDtypeStruct(w.shape, _BF16))
    return in_specs, out_specs, out_shapes


def _cast_blocks(src_refs, dst_refs):
    for src, dst in zip(src_refs, dst_refs):
        dst[...] = src[...].astype(_BF16)


def _conv_mixer_kernel(n_cast, x_ref, xp_ref, xn_ref, g_ref, w_in_ref, cw_ref, cb_ref, w_out_ref, *rest):
    o_ref = rest[n_cast]
    _cast_blocks(rest[:n_cast], rest[n_cast + 1:])
    i = pl.program_id(1)
    n = pl.num_programs(1)
    x = x_ref[0]
    tm = x.shape[0]
    ext = tm + 2 * HALO_ROWS
    lo, hi = HALO_ROWS, HALO_ROWS + tm
    y_ext = _rms_norm(jnp.concatenate([xp_ref[0], x, xn_ref[0]], axis=0), g_ref[...])
    xe = y_ext.astype(_BF16)
    xm = y_ext[lo:hi].astype(_BF16)

    d = D_MODEL
    u_ext = _dot(xe, w_in_ref[:, 0:d]) * _dot(xe, w_in_ref[:, 2 * d:3 * d])
    row = lax.broadcasted_iota(jnp.int32, (ext, 1), 0)
    inside = jnp.logical_and(jnp.logical_or(row >= lo, i > 0), jnp.logical_or(row < hi, i < n - 1))
    u_ext = jnp.where(inside, u_ext, 0.0)
    u_prev = pltpu.roll(u_ext, 1, 0)[lo:hi]
    u_next = pltpu.roll(u_ext, ext - 1, 0)[lo:hi]
    z = cb_ref[...] + u_prev * cw_ref[0:1, :] + u_ext[lo:hi] * cw_ref[1:2, :] + u_next * cw_ref[2:3, :]

    gate_b = _dot(xm, w_in_ref[:, d:2 * d])
    y = _dot((gate_b * z).astype(_BF16), w_out_ref[...])
    o_ref[0] = x + y


def _conv_mixer(x, g, w_in, conv_w, conv_b, w_out, to_cast=()):
    b, s, d = x.shape
    tm = MLP_ROW_TILE
    nt = s // tm
    hb = tm // HALO_ROWS
    last_halo = s // HALO_ROWS - 1
    cast_in, cast_out, cast_shapes = _cast_specs(to_cast, b * nt, lambda bi, i: bi * nt + i)
    return pl.pallas_call(
        functools.partial(_conv_mixer_kernel, len(to_cast)),
        grid=(b, nt),
        in_specs=[
            pl.BlockSpec((1, tm, d), lambda bi, i: (bi, i, 0)),
            pl.BlockSpec((1, HALO_ROWS, d), lambda bi, i: (bi, jnp.maximum(i * hb - 1, 0), 0)),
            pl.BlockSpec((1, HALO_ROWS, d), lambda bi, i: (bi, jnp.minimum((i + 1) * hb, last_halo), 0)),
            _resident((1, d)),
            _resident((d, 3 * d)),
            _resident((CONV_WIDTH, d)),
            _resident((1, d)),
            _resident((d, d)),
        ] + cast_in,
        out_specs=[pl.BlockSpec((1, tm, d), lambda bi, i: (bi, i, 0))] + cast_out,
        out_shape=[jax.ShapeDtypeStruct((b, s, d), _F32)] + cast_shapes,
        compiler_params=_params(("parallel", "parallel")),
        name="conv_mixer",
    )(x, x, x, g, w_in, conv_w, conv_b, w_out, *to_cast)


def _mlp_body(x, g_ref, w_up_ref, w_down_ref):
    xn = _rms_norm(x, g_ref[...]).astype(_BF16)
    acc = x
    for c in range(D_FF // D_MODEL):
        cols = slice(c * D_MODEL, (c + 1) * D_MODEL)
        h = jnp.square(jnp.maximum(_dot(xn, w_up_ref[:, cols]), 0.0)).astype(_BF16)
        acc = acc + _dot(h, w_down_ref[cols, :])
    return acc


def _mlp_kernel(n_cast, x_ref, g_ref, w_up_ref, w_down_ref, *rest):
    o_ref = rest[n_cast]
    _cast_blocks(rest[:n_cast], rest[n_cast + 1:])
    o_ref[...] = _mlp_body(x_ref[...], g_ref, w_up_ref, w_down_ref)


def _proj_mlp_norm_kernel(x_ref, r_ref, sg_ref, w_o_ref, g_ref, w_up_ref, w_down_ref, gf_ref, o_ref):
    proj = None
    for h in range(RET_HEADS):
        cols = slice(h * RET_V_DIM, (h + 1) * RET_V_DIM)
        o = r_ref[:, cols]
        o = o - jnp.mean(o, axis=-1, keepdims=True)
        o = o * lax.rsqrt(jnp.mean(o * o, axis=-1, keepdims=True) + NORM_EPS)
        gated = (sg_ref[:, cols].astype(_F32) * o).astype(_BF16)
        part = _dot(gated, w_o_ref[cols, :])
        proj = part if proj is None else proj + part
    x = x_ref[...] + proj
    y = _mlp_body(x, g_ref, w_up_ref, w_down_ref)
    o_ref[...] = _rms_norm(y, gf_ref[...])


def _mlp(x, g, w_up, w_down, to_cast=()):
    t, d = x.shape
    tm = MLP_ROW_TILE
    rows = pl.BlockSpec((tm, d), lambda i: (i, 0))
    cast_in, cast_out, cast_shapes = _cast_specs(to_cast, t // tm, lambda i: i)
    return pl.pallas_call(
        functools.partial(_mlp_kernel, len(to_cast)),
        grid=(t // tm,),
        in_specs=[rows, _resident((1, d)), _resident((d, D_FF)), _resident((D_FF, d))] + cast_in,
        out_specs=[rows] + cast_out,
        out_shape=[jax.ShapeDtypeStruct((t, d), _F32)] + cast_shapes,
        compiler_params=_params(("parallel",)),
        name="mlp0",
    )(x, g, w_up, w_down, *to_cast)


def _proj_mlp_norm(x, r, sg, w_o, g, w_up, w_down, g_final):
    t, d = x.shape
    tm = ROW_TILE
    rows = pl.BlockSpec((tm, d), lambda i: (i, 0))
    wide = pl.BlockSpec((tm, RET_V_WIDTH), lambda i: (i, 0))
    return pl.pallas_call(
        _proj_mlp_norm_kernel,
        grid=(t // tm,),
        in_specs=[rows, wide, wide, _resident((RET_V_WIDTH, d)),
                  _resident((1, d)), _resident((d, D_FF)), _resident((D_FF, d)), _resident((1, d))],
        out_specs=rows,
        out_shape=jax.ShapeDtypeStruct((t, d), _F32),
        compiler_params=_params(("parallel",)),
        name="proj_mlp1_norm",
    )(x, r, sg, w_o, g, w_up, w_down, g_final)


def _rotary(t, cos, sin):
    half = RET_QK_DIM // 2
    parts = []
    for h in range(RET_HEADS):
        x1 = t[:, h * RET_QK_DIM:h * RET_QK_DIM + half]
        x2 = t[:, h * RET_QK_DIM + half:(h + 1) * RET_QK_DIM]
        parts += [x1 * cos - x2 * sin, x1 * sin + x2 * cos]
    return jnp.concatenate(parts, axis=-1)


def _qkvg_kernel(x_ref, g_ref, w_ref, cos_ref, sin_ref, q_ref, k_ref, v_ref, sg_ref):
    xn = _rms_norm(x_ref[0], g_ref[...]).astype(_BF16)
    cos, sin = cos_ref[...], sin_ref[...]
    qw, vw = RET_QK_WIDTH, RET_V_WIDTH
    gate = _dot(xn, w_ref[:, 2 * qw + vw:2 * qw + 2 * vw])
    sg_ref[0] = jax.nn.silu(gate).astype(_BF16)
    q = _rotary(_dot(xn, w_ref[:, 0:qw]), cos, sin) * (RET_QK_DIM ** -0.5)
    q_ref[0] = q.astype(_BF16)
    k_ref[0] = _rotary(_dot(xn, w_ref[:, qw:2 * qw]), cos, sin).astype(_BF16)
    v_ref[0] = _dot(xn, w_ref[:, 2 * qw:2 * qw + vw]).astype(_BF16)


def _qkvg(x, g, w, cos, sin):
    b, s, d = x.shape
    tm = MLP_ROW_TILE
    half = RET_QK_DIM // 2

    def rows(width):
        return pl.BlockSpec((1, tm, width), lambda bi, i: (bi, i, 0))

    table = pl.BlockSpec((tm, half), lambda bi, i: (i, 0))
    return pl.pallas_call(
        _qkvg_kernel,
        grid=(b, s // tm),
        in_specs=[rows(d), _resident((1, d)), _resident((d, 2 * RET_QK_WIDTH + 2 * RET_V_WIDTH)), table, table],
        out_specs=[rows(RET_QK_WIDTH), rows(RET_QK_WIDTH), rows(RET_V_WIDTH), rows(RET_V_WIDTH)],
        out_shape=[jax.ShapeDtypeStruct((b, s, RET_QK_WIDTH), _BF16),
                   jax.ShapeDtypeStruct((b, s, RET_QK_WIDTH), _BF16),
                   jax.ShapeDtypeStruct((b, s, RET_V_WIDTH), _BF16),
                   jax.ShapeDtypeStruct((b, s, RET_V_WIDTH), _BF16)],
        compiler_params=_params(("parallel", "parallel")),
        name="qkvg_rotary",
    )(x, g, w, cos, sin)


def _retention_kernel(q_ref, k_ref, v_ref, dmat_ref, dqf_ref, dkf_ref, cdf_ref, dqb_ref, dkb_ref, cdb_ref,
                      o_ref, states_ref, sf_ref, sb_ref):
    c = RET_CHUNK
    dk = RET_QK_DIM
    n_chunks = q_ref.shape[1] // c
    contract_rows = (((0,), (0,)), ((), ()))
    contract_cols = (((1,), (1,)), ((), ()))

    def chunk(j):
        return pl.ds(pl.multiple_of(j * c, c), c)

    def decayed_kv(rows, dk_ref):
        kd = (k_ref[0, rows, :].astype(_F32) * dk_ref[0]).astype(_BF16)
        return lax.dot_general(kd, v_ref[0, rows, :], contract_rows, preferred_element_type=_F32)

    sf_ref[...] = jnp.zeros_like(sf_ref)
    sb_ref[...] = jnp.zeros_like(sb_ref)
    states_ref[0, 0:dk, :] = jnp.zeros((dk, RET_V_DIM), _BF16)
    states_ref[n_chunks - 1, dk:2 * dk, :] = jnp.zeros((dk, RET_V_DIM), _BF16)

    def scan(t, carry):
        sf = sf_ref[...] * cdf_ref[0] + decayed_kv(chunk(t), dkf_ref)
        sf_ref[...] = sf
        states_ref[t + 1, 0:dk, :] = sf.astype(_BF16)
        j = n_chunks - 1 - t
        sb = sb_ref[...] * cdb_ref[0] + decayed_kv(chunk(j), dkb_ref)
        sb_ref[...] = sb
        states_ref[j - 1, dk:2 * dk, :] = sb.astype(_BF16)
        return carry

    lax.fori_loop(0, n_chunks - 1, scan, 0, unroll=True)

    def outputs(j, carry):
        rows = chunk(j)
        qc = q_ref[0, rows, :]
        scores = lax.dot_general(qc, k_ref[0, rows, :], contract_cols, preferred_element_type=_F32) * dmat_ref[0]
        qf = qc.astype(_F32)
        qq = jnp.concatenate([(qf * dqf_ref[0]).astype(_BF16), (qf * dqb_ref[0]).astype(_BF16)], axis=1)
        o_ref[0, rows, :] = _dot(scores.astype(_BF16), v_ref[0, rows, :]) + _dot(qq, states_ref[j])
        return carry

    lax.fori_loop(0, n_chunks, outputs, 0, unroll=True)


def _retention_tables():
    c = RET_CHUNK
    h = jnp.arange(RET_HEADS, dtype=_F32)
    lg_f = jnp.log(1.0 - jnp.power(2.0, -5.0 - h))[:, None, None]
    lg_b = jnp.log(1.0 - jnp.power(2.0, -5.5 - h))[:, None, None]
    idx = jnp.arange(c, dtype=_F32)
    diff = idx[:, None] - idx[None, :]
    dmat = jnp.where(diff >= 0, jnp.exp(lg_f * jnp.maximum(diff, 0.0)), jnp.exp(lg_b * jnp.maximum(-diff, 0.0)))
    col = idx[None, :, None]

    def wide(t, width):
        return jnp.broadcast_to(t, (RET_HEADS, t.shape[1], width))

    dqf = wide(jnp.exp(lg_f * (col + 1.0)), RET_QK_DIM)
    dkf = wide(jnp.exp(lg_f * (c - 1.0 - col)), RET_QK_DIM)
    cdf = wide(jnp.exp(lg_f * c), RET_V_DIM)
    dqb = wide(jnp.exp(lg_b * (c - col)), RET_QK_DIM)
    dkb = wide(jnp.exp(lg_b * col), RET_QK_DIM)
    cdb = wide(jnp.exp(lg_b * c), RET_V_DIM)
    return dmat, dqf, dkf, cdf, dqb, dkb, cdb


def _retention(q, k, v, tables):
    b, s, _ = q.shape
    c = RET_CHUNK

    def seq(width):
        return pl.BlockSpec((1, s, width), lambda bi, hi: (bi, 0, hi))

    def per_head(rows, width):
        return pl.BlockSpec((1, rows, width), lambda bi, hi: (hi, 0, 0))

    return pl.pallas_call(
        _retention_kernel,
        grid=(b, RET_HEADS),
        in_specs=[seq(RET_QK_DIM), seq(RET_QK_DIM), seq(RET_V_DIM),
                  per_head(c, c), per_head(c, RET_QK_DIM), per_head(c, RET_QK_DIM), per_head(1, RET_V_DIM),
                  per_head(c, RET_QK_DIM), per_head(c, RET_QK_DIM), per_head(1, RET_V_DIM)],
        out_specs=seq(RET_V_DIM),
        out_shape=jax.ShapeDtypeStruct((b, s, RET_V_WIDTH), _F32),
        scratch_shapes=[pltpu.VMEM((s // c, 2 * RET_QK_DIM, RET_V_DIM), _BF16),
                        pltpu.VMEM((RET_QK_DIM, RET_V_DIM), _F32), pltpu.VMEM((RET_QK_DIM, RET_V_DIM), _F32)],
        compiler_params=_params(("parallel", "arbitrary")),
        name="retention",
    )(q, k, v, *tables)


def _rotary_tables(s):
    half = RET_QK_DIM // 2
    inv = ROPE_BASE ** (-jnp.arange(half, dtype=_F32) / half)
    ang = jnp.arange(s, dtype=_F32)[:, None] * inv[None, :]
    return jnp.cos(ang), jnp.sin(ang)


CAST_IN_CONV = ("w_up_0", "w_down_0", "w_qkvg_1")
CAST_IN_MLP0 = ("w_o_1", "w_up_1", "w_down_1")


def _trunk(x, p, tables, cast_weights):
    b, s, d = x.shape
    p = dict(p)
    names = CAST_IN_CONV if cast_weights else ()
    x, *cast = _conv_mixer(x, p["norm_mix_0"], p["w_in_conv_0"], p["conv_w_0"], p["conv_b_0"], p["w_out_conv_0"],
                           [p[n] for n in names])
    p.update(zip(names, cast))
    names = CAST_IN_MLP0 if cast_weights else ()
    x, *cast = _mlp(x.reshape(b * s, d), p["norm_mlp_0"], p["w_up_0"], p["w_down_0"], [p[n] for n in names])
    p.update(zip(names, cast))
    cos, sin = _rotary_tables(s)
    q, k, v, sg = _qkvg(x.reshape(b, s, d), p["norm_mix_1"], p["w_qkvg_1"], cos, sin)
    r = _retention(q, k, v, tables)
    y = _proj_mlp_norm(x, r.reshape(b * s, RET_V_WIDTH), sg.reshape(b * s, RET_V_WIDTH), p["w_o_1"],
                       p["norm_mlp_1"], p["w_up_1"], p["w_down_1"], p["norm_final"])
    return y.reshape(b, s, d), p


def kernel(x_prompt, x_sample, norm_mix_0, w_in_conv_0, conv_w_0, conv_b_0, w_out_conv_0, norm_mlp_0, w_up_0, w_down_0,
           norm_mix_1, w_qkvg_1, w_o_1, norm_mlp_1, w_up_1, w_down_1, norm_final):
    row = lambda t: t.reshape(1, -1)
    p = {
        "norm_mix_0": row(norm_mix_0), "w_in_conv_0": w_in_conv_0.astype(_BF16), "conv_w_0": conv_w_0,
        "conv_b_0": row(conv_b_0), "w_out_conv_0": w_out_conv_0.astype(_BF16),
        "norm_mlp_0": row(norm_mlp_0), "w_up_0": w_up_0, "w_down_0": w_down_0,
        "norm_mix_1": row(norm_mix_1), "w_qkvg_1": w_qkvg_1, "w_o_1": w_o_1,
        "norm_mlp_1": row(norm_mlp_1), "w_up_1": w_up_1, "w_down_1": w_down_1,
        "norm_final": row(norm_final),
    }
    tables = _retention_tables()
    y_prompt, p = _trunk(x_prompt, p, tables, cast_weights=True)
    y_sample, _ = _trunk(x_sample, p, tables, cast_weights=False)
    return y_prompt, y_sample
```

```python
import functools

import numpy as np
import jax
import jax.numpy as jnp
from jax import lax
from jax.experimental import pallas as pl
from jax.experimental.pallas import tpu as pltpu

D_MODEL = 1024
D_FF = 4 * D_MODEL
CONV_WIDTH = 3
RET_HEADS = 4
RET_QK_DIM = 256
RET_V_DIM = 512
RET_QK_WIDTH = RET_HEADS * RET_QK_DIM
RET_V_WIDTH = RET_HEADS * RET_V_DIM
NORM_EPS = 1e-6
ROPE_BASE = 10000.0

ROW_TILE = 512
MLP_ROW_TILE = 1024
HALO_ROWS = 8
RET_CHUNK = 256
RET_STEP_ROWS = 4096
VMEM_LIMIT_BYTES = 56 * 1024 * 1024

_F32 = jnp.float32
_BF16 = jnp.bfloat16


def _dot(a, b):
    return jnp.dot(a, b, preferred_element_type=_F32)


def _rms_norm(x, g):
    y = x * lax.rsqrt(jnp.mean(x * x, axis=-1, keepdims=True) + NORM_EPS)
    return y * g


def _resident(shape):
    zeros = (0,) * len(shape)
    return pl.BlockSpec(shape, lambda *_: zeros, pipeline_mode=pl.Buffered(1))


def _params(semantics):
    return pltpu.CompilerParams(dimension_semantics=semantics, vmem_limit_bytes=VMEM_LIMIT_BYTES)


def _cast_specs(weights, n_steps, step_of):
    in_specs, out_specs, out_shapes = [], [], []
    for w in weights:
        rows, cols = w.shape
        block = (rows // n_steps, cols)
        in_specs.append(pl.BlockSpec(block, lambda *idx: (step_of(*idx), 0)))
        out_specs.append(pl.BlockSpec(block, lambda *idx: (step_of(*idx), 0)))
        out_shapes.append(jax.ShapeDtypeStruct(w.shape, _BF16))
    return in_specs, out_specs, out_shapes


def _cast_blocks(src_refs, dst_refs):
    for src, dst in zip(src_refs, dst_refs):
        dst[...] = src[...].astype(_BF16)


def _conv_mixer_kernel(n_cast, x_ref, xp_ref, xn_ref, g_ref, w_in_ref, cw_ref, cb_ref, w_out_ref, *rest):
    o_ref = rest[n_cast]
    _cast_blocks(rest[:n_cast], rest[n_cast + 1:])
    i = pl.program_id(1)
    n = pl.num_programs(1)
    x = x_ref[0]
    tm = x.shape[0]
    ext = tm + 2 * HALO_ROWS
    lo, hi = HALO_ROWS, HALO_ROWS + tm
    y_ext = _rms_norm(jnp.concatenate([xp_ref[0], x, xn_ref[0]], axis=0), g_ref[...])
    xe = y_ext.astype(_BF16)
    xm = y_ext[lo:hi].astype(_BF16)

    d = D_MODEL
    u_ext = _dot(xe, w_in_ref[:, 0:d]) * _dot(xe, w_in_ref[:, 2 * d:3 * d])
    row = lax.broadcasted_iota(jnp.int32, (ext, 1), 0)
    inside = jnp.logical_and(jnp.logical_or(row >= lo, i > 0), jnp.logical_or(row < hi, i < n - 1))
    u_ext = jnp.where(inside, u_ext, 0.0)
    u_prev = pltpu.roll(u_ext, 1, 0)[lo:hi]
    u_next = pltpu.roll(u_ext, ext - 1, 0)[lo:hi]
    z = cb_ref[...] + u_prev * cw_ref[0:1, :] + u_ext[lo:hi] * cw_ref[1:2, :] + u_next * cw_ref[2:3, :]

    gate_b = _dot(xm, w_in_ref[:, d:2 * d])
    y = _dot((gate_b * z).astype(_BF16), w_out_ref[...])
    o_ref[0] = x + y


def _conv_mixer(x, g, w_in, conv_w, conv_b, w_out, to_cast=()):
    b, s, d = x.shape
    tm = MLP_ROW_TILE
    nt = s // tm
    hb = tm // HALO_ROWS
    last_halo = s // HALO_ROWS - 1
    cast_in, cast_out, cast_shapes = _cast_specs(to_cast, b * nt, lambda bi, i: bi * nt + i)
    return pl.pallas_call(
        functools.partial(_conv_mixer_kernel, len(to_cast)),
        grid=(b, nt),
        in_specs=[
            pl.BlockSpec((1, tm, d), lambda bi, i: (bi, i, 0)),
            pl.BlockSpec((1, HALO_ROWS, d), lambda bi, i: (bi, jnp.maximum(i * hb - 1, 0), 0)),
            pl.BlockSpec((1, HALO_ROWS, d), lambda bi, i: (bi, jnp.minimum((i + 1) * hb, last_halo), 0)),
            _resident((1, d)),
            _resident((d, 3 * d)),
            _resident((CONV_WIDTH, d)),
            _resident((1, d)),
            _resident((d, d)),
        ] + cast_in,
        out_specs=[pl.BlockSpec((1, tm, d), lambda bi, i: (bi, i, 0))] + cast_out,
        out_shape=[jax.ShapeDtypeStruct((b, s, d), _F32)] + cast_shapes,
        compiler_params=_params(("parallel", "parallel")),
        name="conv_mixer",
    )(x, x, x, g, w_in, conv_w, conv_b, w_out, *to_cast)


def _mlp_body(x, g_ref, w_up_ref, w_down_ref):
    xn = _rms_norm(x, g_ref[...]).astype(_BF16)
    acc = x
    for c in range(D_FF // D_MODEL):
        cols = slice(c * D_MODEL, (c + 1) * D_MODEL)
        h = jnp.square(jnp.maximum(_dot(xn, w_up_ref[:, cols]), 0.0)).astype(_BF16)
        acc = acc + _dot(h, w_down_ref[cols, :])
    return acc


def _mlp_kernel(n_cast, x_ref, g_ref, w_up_ref, w_down_ref, *rest):
    o_ref = rest[n_cast]
    _cast_blocks(rest[:n_cast], rest[n_cast + 1:])
    o_ref[...] = _mlp_body(x_ref[...], g_ref, w_up_ref, w_down_ref)


def _proj_mlp_norm_kernel(x_ref, r_ref, sg_ref, w_o_ref, g_ref, w_up_ref, w_down_ref, gf_ref, o_ref):
    proj = None
    for h in range(RET_HEADS):
        cols = slice(h * RET_V_DIM, (h + 1) * RET_V_DIM)
        o = r_ref[:, cols].astype(_F32)
        o = o - jnp.mean(o, axis=-1, keepdims=True)
        o = o * lax.rsqrt(jnp.mean(o * o, axis=-1, keepdims=True) + NORM_EPS)
        gated = (sg_ref[:, cols].astype(_F32) * o).astype(_BF16)
        part = _dot(gated, w_o_ref[cols, :])
        proj = part if proj is None else proj + part
    x = x_ref[...] + proj
    y = _mlp_body(x, g_ref, w_up_ref, w_down_ref)
    o_ref[...] = _rms_norm(y, gf_ref[...])


def _mlp(x, g, w_up, w_down, to_cast=()):
    t, d = x.shape
    tm = MLP_ROW_TILE
    rows = pl.BlockSpec((tm, d), lambda i: (i, 0))
    cast_in, cast_out, cast_shapes = _cast_specs(to_cast, t // tm, lambda i: i)
    return pl.pallas_call(
        functools.partial(_mlp_kernel, len(to_cast)),
        grid=(t // tm,),
        in_specs=[rows, _resident((1, d)), _resident((d, D_FF)), _resident((D_FF, d))] + cast_in,
        out_specs=[rows] + cast_out,
        out_shape=[jax.ShapeDtypeStruct((t, d), _F32)] + cast_shapes,
        compiler_params=_params(("parallel",)),
        name="mlp0",
    )(x, g, w_up, w_down, *to_cast)


def _proj_mlp_norm(x, r, sg, w_o, g, w_up, w_down, g_final):
    t, d = x.shape
    tm = ROW_TILE
    rows = pl.BlockSpec((tm, d), lambda i: (i, 0))
    wide = pl.BlockSpec((tm, RET_V_WIDTH), lambda i: (i, 0))
    return pl.pallas_call(
        _proj_mlp_norm_kernel,
        grid=(t // tm,),
        in_specs=[rows, wide, wide, _resident((RET_V_WIDTH, d)),
                  _resident((1, d)), _resident((d, D_FF)), _resident((D_FF, d)), _resident((1, d))],
        out_specs=rows,
        out_shape=jax.ShapeDtypeStruct((t, d), _F32),
        compiler_params=_params(("parallel",)),
        name="proj_mlp1_norm",
    )(x, r, sg, w_o, g, w_up, w_down, g_final)


def _rotary(t, cos, sin):
    half = RET_QK_DIM // 2
    parts = []
    for h in range(RET_HEADS):
        x1 = t[:, h * RET_QK_DIM:h * RET_QK_DIM + half]
        x2 = t[:, h * RET_QK_DIM + half:(h + 1) * RET_QK_DIM]
        parts += [x1 * cos - x2 * sin, x1 * sin + x2 * cos]
    return jnp.concatenate(parts, axis=-1)


def _qkvg_kernel(x_ref, g_ref, w_ref, cos_ref, sin_ref, q_ref, k_ref, v_ref, sg_ref):
    xn = _rms_norm(x_ref[0], g_ref[...]).astype(_BF16)
    cos, sin = cos_ref[...], sin_ref[...]
    qw, vw = RET_QK_WIDTH, RET_V_WIDTH
    gate = _dot(xn, w_ref[:, 2 * qw + vw:2 * qw + 2 * vw])
    sg_ref[0] = (gate * (0.5 * jnp.tanh(0.5 * gate) + 0.5)).astype(_BF16)
    q = _rotary(_dot(xn, w_ref[:, 0:qw]), cos, sin) * (RET_QK_DIM ** -0.5)
    q_ref[0] = q.astype(_BF16)
    k_ref[0] = _rotary(_dot(xn, w_ref[:, qw:2 * qw]), cos, sin).astype(_BF16)
    v_ref[0] = _dot(xn, w_ref[:, 2 * qw:2 * qw + vw]).astype(_BF16)


def _qkvg(x, g, w, cos, sin):
    b, s, d = x.shape
    tm = MLP_ROW_TILE
    half = RET_QK_DIM // 2

    def rows(width):
        return pl.BlockSpec((1, tm, width), lambda bi, i: (bi, i, 0))

    table = pl.BlockSpec((tm, half), lambda bi, i: (i, 0))
    return pl.pallas_call(
        _qkvg_kernel,
        grid=(b, s // tm),
        in_specs=[rows(d), _resident((1, d)), _resident((d, 2 * RET_QK_WIDTH + 2 * RET_V_WIDTH)), table, table],
        out_specs=[rows(RET_QK_WIDTH), rows(RET_QK_WIDTH), rows(RET_V_WIDTH), rows(RET_V_WIDTH)],
        out_shape=[jax.ShapeDtypeStruct((b, s, RET_QK_WIDTH), _BF16),
                   jax.ShapeDtypeStruct((b, s, RET_QK_WIDTH), _BF16),
                   jax.ShapeDtypeStruct((b, s, RET_V_WIDTH), _BF16),
                   jax.ShapeDtypeStruct((b, s, RET_V_WIDTH), _BF16)],
        compiler_params=_params(("parallel", "parallel")),
        name="qkvg_rotary",
    )(x, g, w, cos, sin)


def _retention_kernel(q_ref, k_ref, v_ref, dmat_ref, dqf_ref, dkf_ref, cdf_ref, dqb_ref, dkb_ref, cdb_ref,
                      o_ref, states_ref, sf_ref, sb_ref):
    c = RET_CHUNK
    dk = RET_QK_DIM
    n_seq, seq_len, _ = q_ref.shape
    n_chunks = seq_len // c
    contract_rows = (((0,), (0,)), ((), ()))
    contract_cols = (((1,), (1,)), ((), ()))
    zero_state = jnp.zeros((dk, RET_V_DIM), _BF16)

    def chunk(j):
        return slice(j * c, (j + 1) * c)

    def decayed_kv(b, rows, dk_ref):
        kd = (k_ref[b, rows, :].astype(_F32) * dk_ref[0]).astype(_BF16)
        return lax.dot_general(kd, v_ref[b, rows, :], contract_rows, preferred_element_type=_F32)

    for b in range(n_seq):
        sf_ref[b] = jnp.zeros((dk, RET_V_DIM), _F32)
        sb_ref[b] = jnp.zeros((dk, RET_V_DIM), _F32)
        states_ref[b, 0, 0:dk, :] = zero_state
        states_ref[b, n_chunks - 1, dk:2 * dk, :] = zero_state
        for t in range(n_chunks - 1):
            sf = sf_ref[b] * cdf_ref[0] + decayed_kv(b, chunk(t), dkf_ref)
            sf_ref[b] = sf
            states_ref[b, t + 1, 0:dk, :] = sf.astype(_BF16)
            j = n_chunks - 1 - t
            sb = sb_ref[b] * cdb_ref[0] + decayed_kv(b, chunk(j), dkb_ref)
            sb_ref[b] = sb
            states_ref[b, j - 1, dk:2 * dk, :] = sb.astype(_BF16)

    for b in range(n_seq):
        for j in range(n_chunks):
            rows = chunk(j)
            qc = q_ref[b, rows, :]
            scores = lax.dot_general(qc, k_ref[b, rows, :], contract_cols, preferred_element_type=_F32) * dmat_ref[0]
            qf = qc.astype(_F32)
            qq = jnp.concatenate([(qf * dqf_ref[0]).astype(_BF16), (qf * dqb_ref[0]).astype(_BF16)], axis=1)
            o = _dot(scores.astype(_BF16), v_ref[b, rows, :]) + _dot(qq, states_ref[b, j])
            o_ref[b, rows, :] = o.astype(o_ref.dtype)


def _retention_tables():
    c = RET_CHUNK
    f32 = np.float32
    h = np.arange(RET_HEADS, dtype=f32)
    lg_f = np.log(f32(1.0) - np.power(f32(2.0), f32(-5.0) - h)).astype(f32)[:, None, None]
    lg_b = np.log(f32(1.0) - np.power(f32(2.0), f32(-5.5) - h)).astype(f32)[:, None, None]
    idx = np.arange(c, dtype=f32)
    diff = idx[:, None] - idx[None, :]
    dmat = np.where(diff >= 0, np.exp(lg_f * np.maximum(diff, f32(0.0))), np.exp(lg_b * np.maximum(-diff, f32(0.0))))
    col = idx[None, :, None]

    def wide(t, width):
        return jnp.asarray(np.ascontiguousarray(np.broadcast_to(t.astype(f32), (RET_HEADS, t.shape[1], width))))

    dqf = wide(np.exp(lg_f * (col + f32(1.0))), RET_QK_DIM)
    dkf = wide(np.exp(lg_f * (f32(c - 1.0) - col)), RET_QK_DIM)
    cdf = wide(np.exp(lg_f * f32(c)), RET_V_DIM)
    dqb = wide(np.exp(lg_b * (f32(c) - col)), RET_QK_DIM)
    dkb = wide(np.exp(lg_b * col), RET_QK_DIM)
    cdb = wide(np.exp(lg_b * f32(c)), RET_V_DIM)
    return jnp.asarray(dmat.astype(f32)), dqf, dkf, cdf, dqb, dkb, cdb


def _retention(q, k, v, tables):
    b, s, _ = q.shape
    c = RET_CHUNK
    n_seq = max(1, RET_STEP_ROWS // s)

    def seq(width):
        return pl.BlockSpec((n_seq, s, width), lambda bi, hi: (bi, 0, hi))

    def per_head(rows, width):
        return pl.BlockSpec((1, rows, width), lambda bi, hi: (hi, 0, 0))

    state = (RET_QK_DIM, RET_V_DIM)
    return pl.pallas_call(
        _retention_kernel,
        grid=(b // n_seq, RET_HEADS),
        in_specs=[seq(RET_QK_DIM), seq(RET_QK_DIM), seq(RET_V_DIM),
                  per_head(c, c), per_head(c, RET_QK_DIM), per_head(c, RET_QK_DIM), per_head(1, RET_V_DIM),
                  per_head(c, RET_QK_DIM), per_head(c, RET_QK_DIM), per_head(1, RET_V_DIM)],
        out_specs=seq(RET_V_DIM),
        out_shape=jax.ShapeDtypeStruct((b, s, RET_V_WIDTH), _BF16),
        scratch_shapes=[pltpu.VMEM((n_seq, s // c, 2 * RET_QK_DIM, RET_V_DIM), _BF16),
                        pltpu.VMEM((n_seq,) + state, _F32), pltpu.VMEM((n_seq,) + state, _F32)],
        compiler_params=_params(("parallel", "arbitrary")),
        name="retention",
    )(q, k, v, *tables)


def _rotary_tables(s):
    half = RET_QK_DIM // 2
    f32 = np.float32
    inv = np.power(f32(ROPE_BASE), -np.arange(half, dtype=f32) / f32(half)).astype(f32)
    ang = (np.arange(s, dtype=f32)[:, None] * inv[None, :]).astype(f32)
    return jnp.asarray(np.cos(ang).astype(f32)), jnp.asarray(np.sin(ang).astype(f32))


CAST_IN_CONV = ("w_up_0", "w_down_0", "w_qkvg_1")
CAST_IN_MLP0 = ("w_o_1", "w_up_1", "w_down_1")


def _trunk(x, p, tables, cast_weights):
    b, s, d = x.shape
    p = dict(p)
    names = CAST_IN_CONV if cast_weights else ()
    x, *cast = _conv_mixer(x, p["norm_mix_0"], p["w_in_conv_0"], p["conv_w_0"], p["conv_b_0"], p["w_out_conv_0"],
                           [p[n] for n in names])
    p.update(zip(names, cast))
    names = CAST_IN_MLP0 if cast_weights else ()
    x, *cast = _mlp(x.reshape(b * s, d), p["norm_mlp_0"], p["w_up_0"], p["w_down_0"], [p[n] for n in names])
    p.update(zip(names, cast))
    cos, sin = _rotary_tables(s)
    q, k, v, sg = _qkvg(x.reshape(b, s, d), p["norm_mix_1"], p["w_qkvg_1"], cos, sin)
    r = _retention(q, k, v, tables)
    y = _proj_mlp_norm(x, r.reshape(b * s, RET_V_WIDTH), sg.reshape(b * s, RET_V_WIDTH), p["w_o_1"],
                       p["norm_mlp_1"], p["w_up_1"], p["w_down_1"], p["norm_final"])
    return y.reshape(b, s, d), p


def kernel(x_prompt, x_sample, norm_mix_0, w_in_conv_0, conv_w_0, conv_b_0, w_out_conv_0, norm_mlp_0, w_up_0, w_down_0,
           norm_mix_1, w_qkvg_1, w_o_1, norm_mlp_1, w_up_1, w_down_1, norm_final):
    row = lambda t: t.reshape(1, -1)
    p = {
        "norm_mix_0": row(norm_mix_0), "w_in_conv_0": w_in_conv_0.astype(_BF16), "conv_w_0": conv_w_0,
        "conv_b_0": row(conv_b_0), "w_out_conv_0": w_out_conv_0.astype(_BF16),
        "norm_mlp_0": row(norm_mlp_0), "w_up_0": w_up_0, "w_down_0": w_down_0,
        "norm_mix_1": row(norm_mix_1), "w_qkvg_1": w_qkvg_1, "w_o_1": w_o_1,
        "norm_mlp_1": row(norm_mlp_1), "w_up_1": w_up_1, "w_down_1": w_down_1,
        "norm_final": row(norm_final),
    }
    tables = _retention_tables()
    y_prompt, p = _trunk(x_prompt, p, tables, cast_weights=True)
    y_sample, _ = _trunk(x_sample, p, tables, cast_weights=False)
    return y_prompt, y_sample
```

```python
import functools

import numpy as np
import jax
import jax.numpy as jnp
from jax import lax
from jax.experimental import pallas as pl
from jax.experimental.pallas import tpu as pltpu

D_MODEL = 1024
D_FF = 4 * D_MODEL
CONV_WIDTH = 3
RET_HEADS = 4
RET_QK_DIM = 256
RET_V_DIM = 512
RET_QK_WIDTH = RET_HEADS * RET_QK_DIM
RET_V_WIDTH = RET_HEADS * RET_V_DIM
NORM_EPS = 1e-6
ROPE_BASE = 10000.0

ROW_TILE = 512
MLP_ROW_TILE = 1024
HALO_ROWS = 8
RET_CHUNK = 256
RET_STEP_ROWS = 4096
VMEM_LIMIT_BYTES = 56 * 1024 * 1024

_F32 = jnp.float32
_BF16 = jnp.bfloat16


def _dot(a, b):
    return jnp.dot(a, b, preferred_element_type=_F32)


def _rms_norm(x, g):
    y = x * lax.rsqrt(jnp.mean(x * x, axis=-1, keepdims=True) + NORM_EPS)
    return y * g


def _resident(shape):
    zeros = (0,) * len(shape)
    return pl.BlockSpec(shape, lambda *_: zeros, pipeline_mode=pl.Buffered(1))


def _params(semantics):
    return pltpu.CompilerParams(dimension_semantics=semantics, vmem_limit_bytes=VMEM_LIMIT_BYTES)


def _cast_specs(weights, n_steps, step_of):
    in_specs, out_specs, out_shapes = [], [], []
    for w in weights:
        rows, cols = w.shape
        block = (rows // n_steps, cols)
        in_specs.append(pl.BlockSpec(block, lambda *idx: (step_of(*idx), 0)))
        out_specs.append(pl.BlockSpec(block, lambda *idx: (step_of(*idx), 0)))
        out_shapes.append(jax.ShapeDtypeStruct(w.shape, _BF16))
    return in_specs, out_specs, out_shapes


def _cast_blocks(src_refs, dst_refs):
    for src, dst in zip(src_refs, dst_refs):
        dst[...] = src[...].astype(_BF16)


def _conv_mixer_kernel(n_cast, x_ref, xp_ref, xn_ref, g_ref, w_in_ref, cw_ref, cb_ref, w_out_ref, *rest):
    o_ref = rest[n_cast]
    _cast_blocks(rest[:n_cast], rest[n_cast + 1:])
    i = pl.program_id(1)
    n = pl.num_programs(1)
    x = x_ref[0]
    tm = x.shape[0]
    ext = tm + 2 * HALO_ROWS
    lo, hi = HALO_ROWS, HALO_ROWS + tm
    y_ext = _rms_norm(jnp.concatenate([xp_ref[0], x, xn_ref[0]], axis=0), g_ref[...])
    xe = y_ext.astype(_BF16)
    xm = y_ext[lo:hi].astype(_BF16)

    d = D_MODEL
    u_ext = _dot(xe, w_in_ref[:, 0:d]) * _dot(xe, w_in_ref[:, 2 * d:3 * d])
    u_ext = jnp.concatenate([jnp.where(i > 0, u_ext[0:lo], 0.0), u_ext[lo:hi],
                             jnp.where(i < n - 1, u_ext[hi:ext], 0.0)], axis=0)
    u_prev = pltpu.roll(u_ext, 1, 0)[lo:hi]
    u_next = pltpu.roll(u_ext, ext - 1, 0)[lo:hi]
    z = cb_ref[...] + u_prev * cw_ref[0:1, :] + u_ext[lo:hi] * cw_ref[1:2, :] + u_next * cw_ref[2:3, :]

    gate_b = _dot(xm, w_in_ref[:, d:2 * d])
    y = _dot((gate_b * z).astype(_BF16), w_out_ref[...])
    o_ref[0] = x + y


def _conv_mixer(x, g, w_in, conv_w, conv_b, w_out, to_cast=()):
    b, s, d = x.shape
    tm = MLP_ROW_TILE
    nt = s // tm
    hb = tm // HALO_ROWS
    last_halo = s // HALO_ROWS - 1
    cast_in, cast_out, cast_shapes = _cast_specs(to_cast, b * nt, lambda bi, i: bi * nt + i)
    return pl.pallas_call(
        functools.partial(_conv_mixer_kernel, len(to_cast)),
        grid=(b, nt),
        in_specs=[
            pl.BlockSpec((1, tm, d), lambda bi, i: (bi, i, 0)),
            pl.BlockSpec((1, HALO_ROWS, d), lambda bi, i: (bi, jnp.maximum(i * hb - 1, 0), 0)),
            pl.BlockSpec((1, HALO_ROWS, d), lambda bi, i: (bi, jnp.minimum((i + 1) * hb, last_halo), 0)),
            _resident((1, d)),
            _resident((d, 3 * d)),
            _resident((CONV_WIDTH, d)),
            _resident((1, d)),
            _resident((d, d)),
        ] + cast_in,
        out_specs=[pl.BlockSpec((1, tm, d), lambda bi, i: (bi, i, 0))] + cast_out,
        out_shape=[jax.ShapeDtypeStruct((b, s, d), _F32)] + cast_shapes,
        compiler_params=_params(("parallel", "parallel")),
        name="conv_mixer",
    )(x, x, x, g, w_in, conv_w, conv_b, w_out, *to_cast)


def _mlp_body(x, g_ref, w_up_ref, w_down_ref):
    xn = _rms_norm(x, g_ref[...]).astype(_BF16)
    acc = x
    for c in range(D_FF // D_MODEL):
        cols = slice(c * D_MODEL, (c + 1) * D_MODEL)
        h = jnp.square(jnp.maximum(_dot(xn, w_up_ref[:, cols]), 0.0)).astype(_BF16)
        acc = acc + _dot(h, w_down_ref[cols, :])
    return acc


def _mlp_kernel(n_cast, x_ref, g_ref, w_up_ref, w_down_ref, *rest):
    o_ref = rest[n_cast]
    _cast_blocks(rest[:n_cast], rest[n_cast + 1:])
    o_ref[...] = _mlp_body(x_ref[...], g_ref, w_up_ref, w_down_ref)


def _proj_mlp_norm_kernel(x_ref, r_ref, sg_ref, w_o_ref, g_ref, w_up_ref, w_down_ref, gf_ref, o_ref):
    proj = None
    for h in range(RET_HEADS):
        cols = slice(h * RET_V_DIM, (h + 1) * RET_V_DIM)
        o = r_ref[:, cols].astype(_F32)
        o = o - jnp.mean(o, axis=-1, keepdims=True)
        o = o * lax.rsqrt(jnp.mean(o * o, axis=-1, keepdims=True) + NORM_EPS)
        gated = (sg_ref[:, cols].astype(_F32) * o).astype(_BF16)
        part = _dot(gated, w_o_ref[cols, :])
        proj = part if proj is None else proj + part
    x = x_ref[...] + proj
    y = _mlp_body(x, g_ref, w_up_ref, w_down_ref)
    o_ref[...] = _rms_norm(y, gf_ref[...])


def _mlp(x, g, w_up, w_down, to_cast=()):
    t, d = x.shape
    tm = MLP_ROW_TILE
    rows = pl.BlockSpec((tm, d), lambda i: (i, 0))
    cast_in, cast_out, cast_shapes = _cast_specs(to_cast, t // tm, lambda i: i)
    return pl.pallas_call(
        functools.partial(_mlp_kernel, len(to_cast)),
        grid=(t // tm,),
        in_specs=[rows, _resident((1, d)), _resident((d, D_FF)), _resident((D_FF, d))] + cast_in,
        out_specs=[rows] + cast_out,
        out_shape=[jax.ShapeDtypeStruct((t, d), _F32)] + cast_shapes,
        compiler_params=_params(("parallel",)),
        name="mlp0",
    )(x, g, w_up, w_down, *to_cast)


def _proj_mlp_norm(x, r, sg, w_o, g, w_up, w_down, g_final):
    t, d = x.shape
    tm = ROW_TILE
    rows = pl.BlockSpec((tm, d), lambda i: (i, 0))
    wide = pl.BlockSpec((tm, RET_V_WIDTH), lambda i: (i, 0))
    return pl.pallas_call(
        _proj_mlp_norm_kernel,
        grid=(t // tm,),
        in_specs=[rows, wide, wide, _resident((RET_V_WIDTH, d)),
                  _resident((1, d)), _resident((d, D_FF)), _resident((D_FF, d)), _resident((1, d))],
        out_specs=rows,
        out_shape=jax.ShapeDtypeStruct((t, d), _F32),
        compiler_params=_params(("parallel",)),
        name="proj_mlp1_norm",
    )(x, r, sg, w_o, g, w_up, w_down, g_final)


def _rotary(t, cos, sin):
    half = RET_QK_DIM // 2
    parts = []
    for h in range(RET_HEADS):
        x1 = t[:, h * RET_QK_DIM:h * RET_QK_DIM + half]
        x2 = t[:, h * RET_QK_DIM + half:(h + 1) * RET_QK_DIM]
        parts += [x1 * cos - x2 * sin, x1 * sin + x2 * cos]
    return jnp.concatenate(parts, axis=-1)


def _qkvg_kernel(x_ref, g_ref, w_ref, cos_ref, sin_ref, q_ref, k_ref, v_ref, sg_ref):
    xn = _rms_norm(x_ref[0], g_ref[...]).astype(_BF16)
    cos, sin = cos_ref[...], sin_ref[...]
    qw, vw = RET_QK_WIDTH, RET_V_WIDTH
    gate = _dot(xn, w_ref[:, 2 * qw + vw:2 * qw + 2 * vw])
    sg_ref[0] = (gate * (0.5 * jnp.tanh(0.5 * gate) + 0.5)).astype(_BF16)
    q = _rotary(_dot(xn, w_ref[:, 0:qw]), cos, sin) * (RET_QK_DIM ** -0.5)
    q_ref[0] = q.astype(_BF16)
    k_ref[0] = _rotary(_dot(xn, w_ref[:, qw:2 * qw]), cos, sin).astype(_BF16)
    v_ref[0] = _dot(xn, w_ref[:, 2 * qw:2 * qw + vw]).astype(_BF16)


def _qkvg(x, g, w, cos, sin):
    b, s, d = x.shape
    tm = MLP_ROW_TILE
    half = RET_QK_DIM // 2

    def rows(width):
        return pl.BlockSpec((1, tm, width), lambda bi, i: (bi, i, 0))

    table = pl.BlockSpec((tm, half), lambda bi, i: (i, 0))
    return pl.pallas_call(
        _qkvg_kernel,
        grid=(b, s // tm),
        in_specs=[rows(d), _resident((1, d)), _resident((d, 2 * RET_QK_WIDTH + 2 * RET_V_WIDTH)), table, table],
        out_specs=[rows(RET_QK_WIDTH), rows(RET_QK_WIDTH), rows(RET_V_WIDTH), rows(RET_V_WIDTH)],
        out_shape=[jax.ShapeDtypeStruct((b, s, RET_QK_WIDTH), _BF16),
                   jax.ShapeDtypeStruct((b, s, RET_QK_WIDTH), _BF16),
                   jax.ShapeDtypeStruct((b, s, RET_V_WIDTH), _BF16),
                   jax.ShapeDtypeStruct((b, s, RET_V_WIDTH), _BF16)],
        compiler_params=_params(("parallel", "parallel")),
        name="qkvg_rotary",
    )(x, g, w, cos, sin)


def _retention_kernel(q_ref, k_ref, v_ref, dmat_ref, dqf_ref, dkf_ref, cdf_ref, dqb_ref, dkb_ref, cdb_ref, o_ref):
    c = RET_CHUNK
    dk = RET_QK_DIM
    n_seq, seq_len, _ = q_ref.shape
    n_chunks = seq_len // c
    contract_rows = (((0,), (0,)), ((), ()))
    contract_cols = (((1,), (1,)), ((), ()))

    def chunk(j):
        return slice(j * c, (j + 1) * c)

    def decayed_kv(b, rows, dk_ref):
        kd = (k_ref[b, rows, :].astype(_F32) * dk_ref[0]).astype(_BF16)
        return lax.dot_general(kd, v_ref[b, rows, :], contract_rows, preferred_element_type=_F32)

    for b in range(n_seq):
        earlier = [None] * n_chunks
        later = [None] * n_chunks
        sf = sb = None
        for t in range(n_chunks - 1):
            kv = decayed_kv(b, chunk(t), dkf_ref)
            sf = kv if sf is None else sf * cdf_ref[0] + kv
            earlier[t + 1] = sf.astype(_BF16)
            j = n_chunks - 1 - t
            kv = decayed_kv(b, chunk(j), dkb_ref)
            sb = kv if sb is None else sb * cdb_ref[0] + kv
            later[j - 1] = sb.astype(_BF16)

        for j in range(n_chunks):
            rows = chunk(j)
            qc = q_ref[b, rows, :]
            scores = lax.dot_general(qc, k_ref[b, rows, :], contract_cols, preferred_element_type=_F32) * dmat_ref[0]
            o = _dot(scores.astype(_BF16), v_ref[b, rows, :])
            qf = qc.astype(_F32)
            scaled, states = [], []
            if earlier[j] is not None:
                scaled.append((qf * dqf_ref[0]).astype(_BF16))
                states.append(earlier[j])
            if later[j] is not None:
                scaled.append((qf * dqb_ref[0]).astype(_BF16))
                states.append(later[j])
            if scaled:
                o = o + _dot(jnp.concatenate(scaled, axis=1), jnp.concatenate(states, axis=0))
            o_ref[b, rows, :] = o.astype(o_ref.dtype)


def _retention_tables():
    c = RET_CHUNK
    f32 = np.float32
    h = np.arange(RET_HEADS, dtype=f32)
    lg_f = np.log(f32(1.0) - np.power(f32(2.0), f32(-5.0) - h)).astype(f32)[:, None, None]
    lg_b = np.log(f32(1.0) - np.power(f32(2.0), f32(-5.5) - h)).astype(f32)[:, None, None]
    idx = np.arange(c, dtype=f32)
    diff = idx[:, None] - idx[None, :]
    dmat = np.where(diff >= 0, np.exp(lg_f * np.maximum(diff, f32(0.0))), np.exp(lg_b * np.maximum(-diff, f32(0.0))))
    col = idx[None, :, None]

    def wide(t, width):
        return jnp.asarray(np.ascontiguousarray(np.broadcast_to(t.astype(f32), (RET_HEADS, t.shape[1], width))))

    dqf = wide(np.exp(lg_f * (col + f32(1.0))), RET_QK_DIM)
    dkf = wide(np.exp(lg_f * (f32(c - 1.0) - col)), RET_QK_DIM)
    cdf = wide(np.exp(lg_f * f32(c)), RET_V_DIM)
    dqb = wide(np.exp(lg_b * (f32(c) - col)), RET_QK_DIM)
    dkb = wide(np.exp(lg_b * col), RET_QK_DIM)
    cdb = wide(np.exp(lg_b * f32(c)), RET_V_DIM)
    return jnp.asarray(dmat.astype(f32)), dqf, dkf, cdf, dqb, dkb, cdb


def _retention(q, k, v, tables):
    b, s, _ = q.shape
    c = RET_CHUNK
    n_seq = max(1, RET_STEP_ROWS // s)

    def seq(width):
        return pl.BlockSpec((n_seq, s, width), lambda bi, hi: (bi, 0, hi))

    def per_head(rows, width):
        return pl.BlockSpec((1, rows, width), lambda bi, hi: (hi, 0, 0))

    return pl.pallas_call(
        _retention_kernel,
        grid=(b // n_seq, RET_HEADS),
        in_specs=[seq(RET_QK_DIM), seq(RET_QK_DIM), seq(RET_V_DIM),
                  per_head(c, c), per_head(c, RET_QK_DIM), per_head(c, RET_QK_DIM), per_head(1, RET_V_DIM),
                  per_head(c, RET_QK_DIM), per_head(c, RET_QK_DIM), per_head(1, RET_V_DIM)],
        out_specs=seq(RET_V_DIM),
        out_shape=jax.ShapeDtypeStruct((b, s, RET_V_WIDTH), _BF16),
        compiler_params=_params(("parallel", "parallel")),
        name="retention",
    )(q, k, v, *tables)


def _rotary_tables(s):
    half = RET_QK_DIM // 2
    f32 = np.float32
    inv = np.power(f32(ROPE_BASE), -np.arange(half, dtype=f32) / f32(half)).astype(f32)
    ang = (np.arange(s, dtype=f32)[:, None] * inv[None, :]).astype(f32)
    return jnp.asarray(np.cos(ang).astype(f32)), jnp.asarray(np.sin(ang).astype(f32))


CAST_IN_CONV = ("w_up_0", "w_down_0", "w_qkvg_1")
CAST_IN_MLP0 = ("w_o_1", "w_up_1", "w_down_1")


def _trunk(x, p, tables, cast_weights):
    b, s, d = x.shape
    p = dict(p)
    names = CAST_IN_CONV if cast_weights else ()
    x, *cast = _conv_mixer(x, p["norm_mix_0"], p["w_in_conv_0"], p["conv_w_0"], p["conv_b_0"], p["w_out_conv_0"],
                           [p[n] for n in names])
    p.update(zip(names, cast))
    names = CAST_IN_MLP0 if cast_weights else ()
    x, *cast = _mlp(x.reshape(b * s, d), p["norm_mlp_0"], p["w_up_0"], p["w_down_0"], [p[n] for n in names])
    p.update(zip(names, cast))
    cos, sin = _rotary_tables(s)
    q, k, v, sg = _qkvg(x.reshape(b, s, d), p["norm_mix_1"], p["w_qkvg_1"], cos, sin)
    r = _retention(q, k, v, tables)
    y = _proj_mlp_norm(x, r.reshape(b * s, RET_V_WIDTH), sg.reshape(b * s, RET_V_WIDTH), p["w_o_1"],
                       p["norm_mlp_1"], p["w_up_1"], p["w_down_1"], p["norm_final"])
    return y.reshape(b, s, d), p


def kernel(x_prompt, x_sample, norm_mix_0, w_in_conv_0, conv_w_0, conv_b_0, w_out_conv_0, norm_mlp_0, w_up_0, w_down_0,
           norm_mix_1, w_qkvg_1, w_o_1, norm_mlp_1, w_up_1, w_down_1, norm_final):
    row = lambda t: t.reshape(1, -1)
    p = {
        "norm_mix_0": row(norm_mix_0), "w_in_conv_0": w_in_conv_0.astype(_BF16), "conv_w_0": conv_w_0,
        "conv_b_0": row(conv_b_0), "w_out_conv_0": w_out_conv_0.astype(_BF16),
        "norm_mlp_0": row(norm_mlp_0), "w_up_0": w_up_0, "w_down_0": w_down_0,
        "norm_mix_1": row(norm_mix_1), "w_qkvg_1": w_qkvg_1, "w_o_1": w_o_1,
        "norm_mlp_1": row(norm_mlp_1), "w_up_1": w_up_1, "w_down_1": w_down_1,
        "norm_final": row(norm_final),
    }
    tables = _retention_tables()
    y_prompt, p = _trunk(x_prompt, p, tables, cast_weights=True)
    y_sample, _ = _trunk(x_sample, p, tables, cast_weights=False)
    return y_prompt, y_sample
```

```python
import functools

import numpy as np
import jax
import jax.numpy as jnp
from jax import lax
from jax.experimental import pallas as pl
from jax.experimental.pallas import tpu as pltpu

D_MODEL = 1024
D_FF = 4 * D_MODEL
CONV_WIDTH = 3
RET_HEADS = 4
RET_QK_DIM = 256
RET_V_DIM = 512
RET_QK_WIDTH = RET_HEADS * RET_QK_DIM
RET_V_WIDTH = RET_HEADS * RET_V_DIM
NORM_EPS = 1e-6
ROPE_BASE = 10000.0

ROW_TILE = 512
MLP_ROW_TILE = 1024
FF_CHUNK = 1024
HALO_ROWS = 8
RET_CHUNK = 256
RET_STEP_ROWS = 4096
VMEM_LIMIT_BYTES = 56 * 1024 * 1024

_F32 = jnp.float32
_BF16 = jnp.bfloat16


def _dot(a, b):
    return jnp.dot(a, b, preferred_element_type=_F32)


def _rms_norm(x, g):
    y = x * lax.rsqrt(jnp.mean(x * x, axis=-1, keepdims=True) + NORM_EPS)
    return y * g


def _resident(shape):
    zeros = (0,) * len(shape)
    return pl.BlockSpec(shape, lambda *_: zeros, pipeline_mode=pl.Buffered(1))


def _params(semantics):
    return pltpu.CompilerParams(dimension_semantics=semantics, vmem_limit_bytes=VMEM_LIMIT_BYTES)


def _cast_specs(weights, n_steps, step_of):
    in_specs, out_specs, out_shapes = [], [], []
    for w in weights:
        rows, cols = w.shape
        block = (rows // n_steps, cols)
        in_specs.append(pl.BlockSpec(block, lambda *idx: (step_of(*idx), 0)))
        out_specs.append(pl.BlockSpec(block, lambda *idx: (step_of(*idx), 0)))
        out_shapes.append(jax.ShapeDtypeStruct(w.shape, _BF16))
    return in_specs, out_specs, out_shapes


def _cast_blocks(src_refs, dst_refs):
    for src, dst in zip(src_refs, dst_refs):
        dst[...] = src[...].astype(_BF16)


def _conv_mixer_kernel(n_cast, x_ref, xp_ref, xn_ref, g_ref, w_in_ref, cw_ref, cb_ref, w_out_ref, *rest):
    o_ref = rest[n_cast]
    _cast_blocks(rest[:n_cast], rest[n_cast + 1:])
    i = pl.program_id(1)
    n = pl.num_programs(1)
    x = x_ref[0]
    tm = x.shape[0]
    ext = tm + 2 * HALO_ROWS
    lo, hi = HALO_ROWS, HALO_ROWS + tm
    y_ext = _rms_norm(jnp.concatenate([xp_ref[0], x, xn_ref[0]], axis=0), g_ref[...])
    xe = y_ext.astype(_BF16)
    xm = y_ext[lo:hi].astype(_BF16)

    d = D_MODEL
    u_ext = _dot(xe, w_in_ref[:, 0:d]) * _dot(xe, w_in_ref[:, 2 * d:3 * d])
    u_ext = jnp.concatenate([jnp.where(i > 0, u_ext[0:lo], 0.0), u_ext[lo:hi],
                             jnp.where(i < n - 1, u_ext[hi:ext], 0.0)], axis=0)
    u_prev = pltpu.roll(u_ext, 1, 0)[lo:hi]
    u_next = pltpu.roll(u_ext, ext - 1, 0)[lo:hi]
    z = cb_ref[...] + u_prev * cw_ref[0:1, :] + u_ext[lo:hi] * cw_ref[1:2, :] + u_next * cw_ref[2:3, :]

    gate_b = _dot(xm, w_in_ref[:, d:2 * d])
    y = _dot((gate_b * z).astype(_BF16), w_out_ref[...])
    o_ref[0] = x + y


def _conv_mixer(x, g, w_in, conv_w, conv_b, w_out, to_cast=()):
    b, s, d = x.shape
    tm = MLP_ROW_TILE
    nt = s // tm
    hb = tm // HALO_ROWS
    last_halo = s // HALO_ROWS - 1
    cast_in, cast_out, cast_shapes = _cast_specs(to_cast, b * nt, lambda bi, i: bi * nt + i)
    return pl.pallas_call(
        functools.partial(_conv_mixer_kernel, len(to_cast)),
        grid=(b, nt),
        in_specs=[
            pl.BlockSpec((1, tm, d), lambda bi, i: (bi, i, 0)),
            pl.BlockSpec((1, HALO_ROWS, d), lambda bi, i: (bi, jnp.maximum(i * hb - 1, 0), 0)),
            pl.BlockSpec((1, HALO_ROWS, d), lambda bi, i: (bi, jnp.minimum((i + 1) * hb, last_halo), 0)),
            _resident((1, d)),
            _resident((d, 3 * d)),
            _resident((CONV_WIDTH, d)),
            _resident((1, d)),
            _resident((d, d)),
        ] + cast_in,
        out_specs=[pl.BlockSpec((1, tm, d), lambda bi, i: (bi, i, 0))] + cast_out,
        out_shape=[jax.ShapeDtypeStruct((b, s, d), _F32)] + cast_shapes,
        compiler_params=_params(("parallel", "parallel")),
        name="conv_mixer",
    )(x, x, x, g, w_in, conv_w, conv_b, w_out, *to_cast)


def _mlp_body(x, g_ref, w_up_ref, w_down_ref):
    xn = _rms_norm(x, g_ref[...]).astype(_BF16)
    acc = x
    for c in range(D_FF // FF_CHUNK):
        cols = slice(c * FF_CHUNK, (c + 1) * FF_CHUNK)
        h = jnp.square(jnp.maximum(_dot(xn, w_up_ref[:, cols]), 0.0)).astype(_BF16)
        acc = acc + _dot(h, w_down_ref[cols, :])
    return acc


def _mlp_kernel(n_cast, x_ref, g_ref, w_up_ref, w_down_ref, *rest):
    o_ref = rest[n_cast]
    _cast_blocks(rest[:n_cast], rest[n_cast + 1:])
    o_ref[...] = _mlp_body(x_ref[...], g_ref, w_up_ref, w_down_ref)


def _proj_mlp_norm_kernel(x_ref, r_ref, sg_ref, w_o_ref, g_ref, w_up_ref, w_down_ref, gf_ref, o_ref, x3_ref, xn_ref):
    t = pl.program_id(0)
    n_ff = D_FF // FF_CHUNK

    def head_part(h):
        cols = slice(h * RET_V_DIM, (h + 1) * RET_V_DIM)
        o = r_ref[:, cols].astype(_F32)
        o = o - jnp.mean(o, axis=-1, keepdims=True)
        o = o * lax.rsqrt(jnp.mean(o * o, axis=-1, keepdims=True) + NORM_EPS)
        gated = (sg_ref[:, cols].astype(_F32) * o).astype(_BF16)
        return _dot(gated, w_o_ref[cols, :])

    def store_projection(slot, parts):
        x3 = x_ref[...] + functools.reduce(lambda a, b: a + b, parts)
        x3_ref[slot] = x3
        xn_ref[slot] = _rms_norm(x3, g_ref[...]).astype(_BF16)

    def step(read, write):
        x3, xn = x3_ref[read], xn_ref[read]
        acc, parts = x3, []
        for c in range(n_ff):
            cols = slice(c * FF_CHUNK, (c + 1) * FF_CHUNK)
            hidden = jnp.square(jnp.maximum(_dot(xn, w_up_ref[:, cols]), 0.0)).astype(_BF16)
            last = c == n_ff - 1
            parts += [head_part(h) for h in range(c, RET_HEADS if last else c + 1)]
            if last:
                store_projection(write, parts)
            acc = acc + _dot(hidden, w_down_ref[cols, :])
        o_ref[...] = _rms_norm(acc, gf_ref[...])

    @pl.when(t == 0)
    def _():
        store_projection(0, [head_part(h) for h in range(RET_HEADS)])

    @pl.when(jnp.logical_and(t > 0, t % 2 == 1))
    def _():
        step(read=0, write=1)

    @pl.when(jnp.logical_and(t > 0, t % 2 == 0))
    def _():
        step(read=1, write=0)


def _mlp(x, g, w_up, w_down, to_cast=()):
    t, d = x.shape
    tm = MLP_ROW_TILE
    rows = pl.BlockSpec((tm, d), lambda i: (i, 0))
    cast_in, cast_out, cast_shapes = _cast_specs(to_cast, t // tm, lambda i: i)
    return pl.pallas_call(
        functools.partial(_mlp_kernel, len(to_cast)),
        grid=(t // tm,),
        in_specs=[rows, _resident((1, d)), _resident((d, D_FF)), _resident((D_FF, d))] + cast_in,
        out_specs=[rows] + cast_out,
        out_shape=[jax.ShapeDtypeStruct((t, d), _F32)] + cast_shapes,
        compiler_params=_params(("parallel",)),
        name="mlp0",
    )(x, g, w_up, w_down, *to_cast)


def _proj_mlp_norm(x, r, sg, w_o, g, w_up, w_down, g_final):
    t, d = x.shape
    tm = ROW_TILE
    n_tiles = t // tm
    rows_in = pl.BlockSpec((tm, d), lambda i: (jnp.minimum(i, n_tiles - 1), 0))
    wide_in = pl.BlockSpec((tm, RET_V_WIDTH), lambda i: (jnp.minimum(i, n_tiles - 1), 0))
    rows_out = pl.BlockSpec((tm, d), lambda i: (jnp.maximum(i - 1, 0), 0))
    return pl.pallas_call(
        _proj_mlp_norm_kernel,
        grid=(n_tiles + 1,),
        in_specs=[rows_in, wide_in, wide_in, _resident((RET_V_WIDTH, d)),
                  _resident((1, d)), _resident((d, D_FF)), _resident((D_FF, d)), _resident((1, d))],
        out_specs=rows_out,
        out_shape=jax.ShapeDtypeStruct((t, d), _F32),
        scratch_shapes=[pltpu.VMEM((2, tm, d), _F32), pltpu.VMEM((2, tm, d), _BF16)],
        compiler_params=_params(("arbitrary",)),
        name="proj_mlp1_norm",
    )(x, r, sg, w_o, g, w_up, w_down, g_final)


def _rotary(t, cos, sin):
    half = RET_QK_DIM // 2
    parts = []
    for h in range(RET_HEADS):
        x1 = t[:, h * RET_QK_DIM:h * RET_QK_DIM + half]
        x2 = t[:, h * RET_QK_DIM + half:(h + 1) * RET_QK_DIM]
        parts += [x1 * cos - x2 * sin, x1 * sin + x2 * cos]
    return jnp.concatenate(parts, axis=-1)


def _qkvg_kernel(x_ref, g_ref, w_ref, cos_ref, sin_ref, q_ref, k_ref, v_ref, sg_ref):
    xn = _rms_norm(x_ref[0], g_ref[...]).astype(_BF16)
    cos, sin = cos_ref[...], sin_ref[...]
    qw, vw = RET_QK_WIDTH, RET_V_WIDTH
    gate = _dot(xn, w_ref[:, 2 * qw + vw:2 * qw + 2 * vw])
    sg_ref[0] = (gate * (0.5 * jnp.tanh(0.5 * gate) + 0.5)).astype(_BF16)
    q = _rotary(_dot(xn, w_ref[:, 0:qw]), cos, sin) * (RET_QK_DIM ** -0.5)
    q_ref[0] = q.astype(_BF16)
    k_ref[0] = _rotary(_dot(xn, w_ref[:, qw:2 * qw]), cos, sin).astype(_BF16)
    v_ref[0] = _dot(xn, w_ref[:, 2 * qw:2 * qw + vw]).astype(_BF16)


def _qkvg(x, g, w, cos, sin):
    b, s, d = x.shape
    tm = MLP_ROW_TILE
    half = RET_QK_DIM // 2

    def rows(width):
        return pl.BlockSpec((1, tm, width), lambda bi, i: (bi, i, 0))

    table = pl.BlockSpec((tm, half), lambda bi, i: (i, 0))
    return pl.pallas_call(
        _qkvg_kernel,
        grid=(b, s // tm),
        in_specs=[rows(d), _resident((1, d)), _resident((d, 2 * RET_QK_WIDTH + 2 * RET_V_WIDTH)), table, table],
        out_specs=[rows(RET_QK_WIDTH), rows(RET_QK_WIDTH), rows(RET_V_WIDTH), rows(RET_V_WIDTH)],
        out_shape=[jax.ShapeDtypeStruct((b, s, RET_QK_WIDTH), _BF16),
                   jax.ShapeDtypeStruct((b, s, RET_QK_WIDTH), _BF16),
                   jax.ShapeDtypeStruct((b, s, RET_V_WIDTH), _BF16),
                   jax.ShapeDtypeStruct((b, s, RET_V_WIDTH), _BF16)],
        compiler_params=_params(("parallel", "parallel")),
        name="qkvg_rotary",
    )(x, g, w, cos, sin)


def _retention_kernel(q_ref, k_ref, v_ref, dmat_ref, dqf_ref, dkf_ref, cdf_ref, dqb_ref, dkb_ref, cdb_ref, o_ref):
    c = RET_CHUNK
    dk = RET_QK_DIM
    n_seq, seq_len, _ = q_ref.shape
    n_chunks = seq_len // c
    contract_rows = (((0,), (0,)), ((), ()))
    contract_cols = (((1,), (1,)), ((), ()))

    def chunk(j):
        return slice(j * c, (j + 1) * c)

    def decayed_kv(b, rows, dk_ref):
        kd = (k_ref[b, rows, :].astype(_F32) * dk_ref[0]).astype(_BF16)
        return lax.dot_general(kd, v_ref[b, rows, :], contract_rows, preferred_element_type=_F32)

    for b in range(n_seq):
        earlier = [None] * n_chunks
        later = [None] * n_chunks
        sf = sb = None
        for t in range(n_chunks - 1):
            kv = decayed_kv(b, chunk(t), dkf_ref)
            sf = kv if sf is None else sf * cdf_ref[0] + kv
            earlier[t + 1] = sf.astype(_BF16)
            j = n_chunks - 1 - t
            kv = decayed_kv(b, chunk(j), dkb_ref)
            sb = kv if sb is None else sb * cdb_ref[0] + kv
            later[j - 1] = sb.astype(_BF16)

        for j in range(n_chunks):
            rows = chunk(j)
            qc = q_ref[b, rows, :]
            scores = lax.dot_general(qc, k_ref[b, rows, :], contract_cols, preferred_element_type=_F32) * dmat_ref[0]
            o = _dot(scores.astype(_BF16), v_ref[b, rows, :])
            qf = qc.astype(_F32)
            scaled, states = [], []
            if earlier[j] is not None:
                scaled.append((qf * dqf_ref[0]).astype(_BF16))
                states.append(earlier[j])
            if later[j] is not None:
                scaled.append((qf * dqb_ref[0]).astype(_BF16))
                states.append(later[j])
            if scaled:
                o = o + _dot(jnp.concatenate(scaled, axis=1), jnp.concatenate(states, axis=0))
            o_ref[b, rows, :] = o.astype(o_ref.dtype)


def _retention_tables():
    c = RET_CHUNK
    f32 = np.float32
    h = np.arange(RET_HEADS, dtype=f32)
    lg_f = np.log(f32(1.0) - np.power(f32(2.0), f32(-5.0) - h)).astype(f32)[:, None, None]
    lg_b = np.log(f32(1.0) - np.power(f32(2.0), f32(-5.5) - h)).astype(f32)[:, None, None]
    idx = np.arange(c, dtype=f32)
    diff = idx[:, None] - idx[None, :]
    dmat = np.where(diff >= 0, np.exp(lg_f * np.maximum(diff, f32(0.0))), np.exp(lg_b * np.maximum(-diff, f32(0.0))))
    col = idx[None, :, None]

    def wide(t, width):
        return jnp.asarray(np.ascontiguousarray(np.broadcast_to(t.astype(f32), (RET_HEADS, t.shape[1], width))))

    dqf = wide(np.exp(lg_f * (col + f32(1.0))), RET_QK_DIM)
    dkf = wide(np.exp(lg_f * (f32(c - 1.0) - col)), RET_QK_DIM)
    cdf = wide(np.exp(lg_f * f32(c)), RET_V_DIM)
    dqb = wide(np.exp(lg_b * (f32(c) - col)), RET_QK_DIM)
    dkb = wide(np.exp(lg_b * col), RET_QK_DIM)
    cdb = wide(np.exp(lg_b * f32(c)), RET_V_DIM)
    return jnp.asarray(dmat.astype(f32)), dqf, dkf, cdf, dqb, dkb, cdb


def _retention(q, k, v, tables):
    b, s, _ = q.shape
    c = RET_CHUNK
    n_seq = max(1, RET_STEP_ROWS // s)

    def seq(width):
        return pl.BlockSpec((n_seq, s, width), lambda bi, hi: (bi, 0, hi))

    def per_head(rows, width):
        return pl.BlockSpec((1, rows, width), lambda bi, hi: (hi, 0, 0))

    return pl.pallas_call(
        _retention_kernel,
        grid=(b // n_seq, RET_HEADS),
        in_specs=[seq(RET_QK_DIM), seq(RET_QK_DIM), seq(RET_V_DIM),
                  per_head(c, c), per_head(c, RET_QK_DIM), per_head(c, RET_QK_DIM), per_head(1, RET_V_DIM),
                  per_head(c, RET_QK_DIM), per_head(c, RET_QK_DIM), per_head(1, RET_V_DIM)],
        out_specs=seq(RET_V_DIM),
        out_shape=jax.ShapeDtypeStruct((b, s, RET_V_WIDTH), _BF16),
        compiler_params=_params(("parallel", "parallel")),
        name="retention",
    )(q, k, v, *tables)


def _rotary_tables(s):
    half = RET_QK_DIM // 2
    f32 = np.float32
    inv = np.power(f32(ROPE_BASE), -np.arange(half, dtype=f32) / f32(half)).astype(f32)
    ang = (np.arange(s, dtype=f32)[:, None] * inv[None, :]).astype(f32)
    return jnp.asarray(np.cos(ang).astype(f32)), jnp.asarray(np.sin(ang).astype(f32))


CAST_IN_CONV = ("w_up_0", "w_down_0", "w_qkvg_1")
CAST_IN_MLP0 = ("w_o_1", "w_up_1", "w_down_1")


def _trunk(x, p, tables, cast_weights):
    b, s, d = x.shape
    p = dict(p)
    names = CAST_IN_CONV if cast_weights else ()
    x, *cast = _conv_mixer(x, p["norm_mix_0"], p["w_in_conv_0"], p["conv_w_0"], p["conv_b_0"], p["w_out_conv_0"],
                           [p[n] for n in names])
    p.update(zip(names, cast))
    names = CAST_IN_MLP0 if cast_weights else ()
    x, *cast = _mlp(x.reshape(b * s, d), p["norm_mlp_0"], p["w_up_0"], p["w_down_0"], [p[n] for n in names])
    p.update(zip(names, cast))
    cos, sin = _rotary_tables(s)
    q, k, v, sg = _qkvg(x.reshape(b, s, d), p["norm_mix_1"], p["w_qkvg_1"], cos, sin)
    r = _retention(q, k, v, tables)
    y = _proj_mlp_norm(x, r.reshape(b * s, RET_V_WIDTH), sg.reshape(b * s, RET_V_WIDTH), p["w_o_1"],
                       p["norm_mlp_1"], p["w_up_1"], p["w_down_1"], p["norm_final"])
    return y.reshape(b, s, d), p


def kernel(x_prompt, x_sample, norm_mix_0, w_in_conv_0, conv_w_0, conv_b_0, w_out_conv_0, norm_mlp_0, w_up_0, w_down_0,
           norm_mix_1, w_qkvg_1, w_o_1, norm_mlp_1, w_up_1, w_down_1, norm_final):
    row = lambda t: t.reshape(1, -1)
    p = {
        "norm_mix_0": row(norm_mix_0), "w_in_conv_0": w_in_conv_0.astype(_BF16), "conv_w_0": conv_w_0,
        "conv_b_0": row(conv_b_0), "w_out_conv_0": w_out_conv_0.astype(_BF16),
        "norm_mlp_0": row(norm_mlp_0), "w_up_0": w_up_0, "w_down_0": w_down_0,
        "norm_mix_1": row(norm_mix_1), "w_qkvg_1": w_qkvg_1, "w_o_1": w_o_1,
        "norm_mlp_1": row(norm_mlp_1), "w_up_1": w_up_1, "w_down_1": w_down_1,
        "norm_final": row(norm_final),
    }
    tables = _retention_tables()
    y_prompt, p = _trunk(x_prompt, p, tables, cast_weights=True)
    y_sample, _ = _trunk(x_sample, p, tables, cast_weights=False)
    return y_prompt, y_sample
```

```python
import functools

import numpy as np
import jax
import jax.numpy as jnp
from jax import lax
from jax.experimental import pallas as pl
from jax.experimental.pallas import tpu as pltpu

D_MODEL = 1024
D_FF = 4 * D_MODEL
CONV_WIDTH = 3
RET_HEADS = 4
RET_QK_DIM = 256
RET_V_DIM = 512
RET_QK_WIDTH = RET_HEADS * RET_QK_DIM
RET_V_WIDTH = RET_HEADS * RET_V_DIM
NORM_EPS = 1e-6
ROPE_BASE = 10000.0

ROW_TILE = 512
MLP_ROW_TILE = 1024
FF_CHUNK = 1024
HALO_ROWS = 8
RET_CHUNK = 256
RET_STEP_ROWS = 4096
VMEM_LIMIT_BYTES = 56 * 1024 * 1024

_F32 = jnp.float32
_BF16 = jnp.bfloat16


def _dot(a, b):
    return jnp.dot(a, b, preferred_element_type=_F32)


def _rms_norm(x, g):
    y = x * lax.rsqrt(jnp.mean(x * x, axis=-1, keepdims=True) + NORM_EPS)
    return y * g


def _resident(shape):
    zeros = (0,) * len(shape)
    return pl.BlockSpec(shape, lambda *_: zeros, pipeline_mode=pl.Buffered(1))


def _params(semantics):
    return pltpu.CompilerParams(dimension_semantics=semantics, vmem_limit_bytes=VMEM_LIMIT_BYTES)


def _cast_specs(weights, n_steps, step_of):
    in_specs, out_specs, out_shapes = [], [], []
    for w in weights:
        rows, cols = w.shape
        block = (rows // n_steps, cols)
        in_specs.append(pl.BlockSpec(block, lambda *idx: (step_of(*idx), 0)))
        out_specs.append(pl.BlockSpec(block, lambda *idx: (step_of(*idx), 0)))
        out_shapes.append(jax.ShapeDtypeStruct(w.shape, _BF16))
    return in_specs, out_specs, out_shapes


def _cast_blocks(src_refs, dst_refs):
    for src, dst in zip(src_refs, dst_refs):
        dst[...] = src[...].astype(_BF16)


def _conv_mixer_kernel(n_cast, x_ref, xp_ref, xn_ref, g_ref, w_in_ref, cw_ref, cb_ref, w_out_ref, *rest):
    o_ref, u_ref = rest[n_cast], rest[-1]
    _cast_blocks(rest[:n_cast], rest[n_cast + 1:-1])
    i = pl.program_id(1)
    n = pl.num_programs(1)
    x = x_ref[0]
    tm = x.shape[0]
    ext = tm + 2 * HALO_ROWS
    lo, hi = HALO_ROWS, HALO_ROWS + tm
    y_ext = _rms_norm(jnp.concatenate([xp_ref[0], x, xn_ref[0]], axis=0), g_ref[...])
    xe = y_ext.astype(_BF16)
    xm = y_ext[lo:hi].astype(_BF16)

    d = D_MODEL
    u_ext = _dot(xe, w_in_ref[:, 0:d]) * _dot(xe, w_in_ref[:, 2 * d:3 * d])
    u_ref[0:lo, :] = jnp.where(i > 0, u_ext[0:lo], 0.0)
    u_ref[lo:hi, :] = u_ext[lo:hi]
    u_ref[hi:ext, :] = jnp.where(i < n - 1, u_ext[hi:ext], 0.0)
    u_prev = u_ref[lo - 1:hi - 1, :]
    u_next = u_ref[lo + 1:hi + 1, :]
    z = cb_ref[...] + u_prev * cw_ref[0:1, :] + u_ext[lo:hi] * cw_ref[1:2, :] + u_next * cw_ref[2:3, :]

    gate_b = _dot(xm, w_in_ref[:, d:2 * d])
    y = _dot((gate_b * z).astype(_BF16), w_out_ref[...])
    o_ref[0] = x + y


def _conv_mixer(x, g, w_in, conv_w, conv_b, w_out, to_cast=()):
    b, s, d = x.shape
    tm = MLP_ROW_TILE
    nt = s // tm
    hb = tm // HALO_ROWS
    last_halo = s // HALO_ROWS - 1
    cast_in, cast_out, cast_shapes = _cast_specs(to_cast, b * nt, lambda bi, i: bi * nt + i)
    return pl.pallas_call(
        functools.partial(_conv_mixer_kernel, len(to_cast)),
        grid=(b, nt),
        in_specs=[
            pl.BlockSpec((1, tm, d), lambda bi, i: (bi, i, 0)),
            pl.BlockSpec((1, HALO_ROWS, d), lambda bi, i: (bi, jnp.maximum(i * hb - 1, 0), 0)),
            pl.BlockSpec((1, HALO_ROWS, d), lambda bi, i: (bi, jnp.minimum((i + 1) * hb, last_halo), 0)),
            _resident((1, d)),
            _resident((d, 3 * d)),
            _resident((CONV_WIDTH, d)),
            _resident((1, d)),
            _resident((d, d)),
        ] + cast_in,
        out_specs=[pl.BlockSpec((1, tm, d), lambda bi, i: (bi, i, 0))] + cast_out,
        out_shape=[jax.ShapeDtypeStruct((b, s, d), _F32)] + cast_shapes,
        scratch_shapes=[pltpu.VMEM((tm + 2 * HALO_ROWS, d), _F32)],
        compiler_params=_params(("parallel", "parallel")),
        name="conv_mixer",
    )(x, x, x, g, w_in, conv_w, conv_b, w_out, *to_cast)


def _mlp_body(x, g_ref, w_up_ref, w_down_ref):
    xn = _rms_norm(x, g_ref[...]).astype(_BF16)
    acc = x
    for c in range(D_FF // FF_CHUNK):
        cols = slice(c * FF_CHUNK, (c + 1) * FF_CHUNK)
        h = jnp.square(jnp.maximum(_dot(xn, w_up_ref[:, cols]), 0.0)).astype(_BF16)
        acc = acc + _dot(h, w_down_ref[cols, :])
    return acc


def _mlp_kernel(n_cast, x_ref, g_ref, w_up_ref, w_down_ref, *rest):
    o_ref = rest[n_cast]
    _cast_blocks(rest[:n_cast], rest[n_cast + 1:])
    o_ref[...] = _mlp_body(x_ref[...], g_ref, w_up_ref, w_down_ref)


def _proj_mlp_norm_kernel(x_ref, r_ref, sg_ref, w_o_ref, g_ref, w_up_ref, w_down_ref, gf_ref, o_ref):
    proj = None
    for h in range(RET_HEADS):
        cols = slice(h * RET_V_DIM, (h + 1) * RET_V_DIM)
        o = r_ref[:, cols].astype(_F32)
        o = o - jnp.mean(o, axis=-1, keepdims=True)
        o = o * lax.rsqrt(jnp.mean(o * o, axis=-1, keepdims=True) + NORM_EPS)
        gated = (sg_ref[:, cols].astype(_F32) * o).astype(_BF16)
        part = _dot(gated, w_o_ref[cols, :])
        proj = part if proj is None else proj + part
    x = x_ref[...] + proj
    y = _mlp_body(x, g_ref, w_up_ref, w_down_ref)
    o_ref[...] = _rms_norm(y, gf_ref[...])


def _mlp(x, g, w_up, w_down, to_cast=()):
    t, d = x.shape
    tm = MLP_ROW_TILE
    rows = pl.BlockSpec((tm, d), lambda i: (i, 0))
    cast_in, cast_out, cast_shapes = _cast_specs(to_cast, t // tm, lambda i: i)
    return pl.pallas_call(
        functools.partial(_mlp_kernel, len(to_cast)),
        grid=(t // tm,),
        in_specs=[rows, _resident((1, d)), _resident((d, D_FF)), _resident((D_FF, d))] + cast_in,
        out_specs=[rows] + cast_out,
        out_shape=[jax.ShapeDtypeStruct((t, d), _F32)] + cast_shapes,
        compiler_params=_params(("parallel",)),
        name="mlp0",
    )(x, g, w_up, w_down, *to_cast)


def _proj_mlp_norm(x, r, sg, w_o, g, w_up, w_down, g_final):
    t, d = x.shape
    tm = ROW_TILE
    rows = pl.BlockSpec((tm, d), lambda i: (i, 0))
    wide = pl.BlockSpec((tm, RET_V_WIDTH), lambda i: (i, 0))
    return pl.pallas_call(
        _proj_mlp_norm_kernel,
        grid=(t // tm,),
        in_specs=[rows, wide, wide, _resident((RET_V_WIDTH, d)),
                  _resident((1, d)), _resident((d, D_FF)), _resident((D_FF, d)), _resident((1, d))],
        out_specs=rows,
        out_shape=jax.ShapeDtypeStruct((t, d), _F32),
        compiler_params=_params(("parallel",)),
        name="proj_mlp1_norm",
    )(x, r, sg, w_o, g, w_up, w_down, g_final)


def _rotary(t, cos, sin):
    half = RET_QK_DIM // 2
    parts = []
    for h in range(RET_HEADS):
        x1 = t[:, h * RET_QK_DIM:h * RET_QK_DIM + half]
        x2 = t[:, h * RET_QK_DIM + half:(h + 1) * RET_QK_DIM]
        parts += [x1 * cos - x2 * sin, x1 * sin + x2 * cos]
    return jnp.concatenate(parts, axis=-1)


def _qkvg_kernel(x_ref, g_ref, w_ref, cos_ref, sin_ref, q_ref, k_ref, v_ref, sg_ref):
    xn = _rms_norm(x_ref[0], g_ref[...]).astype(_BF16)
    cos, sin = cos_ref[...], sin_ref[...]
    qw, vw = RET_QK_WIDTH, RET_V_WIDTH
    gate = _dot(xn, w_ref[:, 2 * qw + vw:2 * qw + 2 * vw])
    sg_ref[0] = (gate * (0.5 * jnp.tanh(0.5 * gate) + 0.5)).astype(_BF16)
    q = _rotary(_dot(xn, w_ref[:, 0:qw]), cos, sin) * (RET_QK_DIM ** -0.5)
    q_ref[0] = q.astype(_BF16)
    k_ref[0] = _rotary(_dot(xn, w_ref[:, qw:2 * qw]), cos, sin).astype(_BF16)
    v_ref[0] = _dot(xn, w_ref[:, 2 * qw:2 * qw + vw]).astype(_BF16)


def _qkvg(x, g, w, cos, sin):
    b, s, d = x.shape
    tm = MLP_ROW_TILE
    half = RET_QK_DIM // 2

    def rows(width):
        return pl.BlockSpec((1, tm, width), lambda bi, i: (bi, i, 0))

    table = pl.BlockSpec((tm, half), lambda bi, i: (i, 0))
    return pl.pallas_call(
        _qkvg_kernel,
        grid=(b, s // tm),
        in_specs=[rows(d), _resident((1, d)), _resident((d, 2 * RET_QK_WIDTH + 2 * RET_V_WIDTH)), table, table],
        out_specs=[rows(RET_QK_WIDTH), rows(RET_QK_WIDTH), rows(RET_V_WIDTH), rows(RET_V_WIDTH)],
        out_shape=[jax.ShapeDtypeStruct((b, s, RET_QK_WIDTH), _BF16),
                   jax.ShapeDtypeStruct((b, s, RET_QK_WIDTH), _BF16),
                   jax.ShapeDtypeStruct((b, s, RET_V_WIDTH), _BF16),
                   jax.ShapeDtypeStruct((b, s, RET_V_WIDTH), _BF16)],
        compiler_params=_params(("parallel", "parallel")),
        name="qkvg_rotary",
    )(x, g, w, cos, sin)


def _retention_kernel(q_ref, k_ref, v_ref, dmat_ref, dqf_ref, dkf_ref, cdf_ref, dqb_ref, dkb_ref, cdb_ref, o_ref):
    c = RET_CHUNK
    dk = RET_QK_DIM
    n_seq, seq_len, _ = q_ref.shape
    n_chunks = seq_len // c
    contract_rows = (((0,), (0,)), ((), ()))
    contract_cols = (((1,), (1,)), ((), ()))

    def chunk(j):
        return slice(j * c, (j + 1) * c)

    def decayed_kv(b, rows, dk_ref):
        kd = (k_ref[b, rows, :].astype(_F32) * dk_ref[0]).astype(_BF16)
        return lax.dot_general(kd, v_ref[b, rows, :], contract_rows, preferred_element_type=_F32)

    def emit(b, j, earlier, later):
        rows = chunk(j)
        qc = q_ref[b, rows, :]
        scores = lax.dot_general(qc, k_ref[b, rows, :], contract_cols, preferred_element_type=_F32) * dmat_ref[0]
        o = _dot(scores.astype(_BF16), v_ref[b, rows, :])
        qf = qc.astype(_F32)
        scaled, states = [], []
        if earlier is not None:
            scaled.append((qf * dqf_ref[0]).astype(_BF16))
            states.append(earlier)
        if later is not None:
            scaled.append((qf * dqb_ref[0]).astype(_BF16))
            states.append(later)
        if scaled:
            o = o + _dot(jnp.concatenate(scaled, axis=1), jnp.concatenate(states, axis=0))
        o_ref[b, rows, :] = o.astype(o_ref.dtype)

    for b in range(n_seq):
        earlier = [None] * n_chunks
        later = [None] * n_chunks
        sf = sb = None
        for t in range(n_chunks - 1):
            kv = decayed_kv(b, chunk(t), dkf_ref)
            sf = kv if sf is None else sf * cdf_ref[0] + kv
            earlier[t + 1] = sf.astype(_BF16)
            j = n_chunks - 1 - t
            kv = decayed_kv(b, chunk(j), dkb_ref)
            sb = kv if sb is None else sb * cdb_ref[0] + kv
            later[j - 1] = sb.astype(_BF16)
        for j in range(n_chunks):
            emit(b, j, earlier[j], later[j])


def _retention_tables():
    c = RET_CHUNK
    f32 = np.float32
    h = np.arange(RET_HEADS, dtype=f32)
    lg_f = np.log(f32(1.0) - np.power(f32(2.0), f32(-5.0) - h)).astype(f32)[:, None, None]
    lg_b = np.log(f32(1.0) - np.power(f32(2.0), f32(-5.5) - h)).astype(f32)[:, None, None]
    idx = np.arange(c, dtype=f32)
    diff = idx[:, None] - idx[None, :]
    dmat = np.where(diff >= 0, np.exp(lg_f * np.maximum(diff, f32(0.0))), np.exp(lg_b * np.maximum(-diff, f32(0.0))))
    col = idx[None, :, None]

    def wide(t, width):
        return jnp.asarray(np.ascontiguousarray(np.broadcast_to(t.astype(f32), (RET_HEADS, t.shape[1], width))))

    dqf = wide(np.exp(lg_f * (col + f32(1.0))), RET_QK_DIM)
    dkf = wide(np.exp(lg_f * (f32(c - 1.0) - col)), RET_QK_DIM)
    cdf = wide(np.exp(lg_f * f32(c)), RET_V_DIM)
    dqb = wide(np.exp(lg_b * (f32(c) - col)), RET_QK_DIM)
    dkb = wide(np.exp(lg_b * col), RET_QK_DIM)
    cdb = wide(np.exp(lg_b * f32(c)), RET_V_DIM)
    return jnp.asarray(dmat.astype(f32)), dqf, dkf, cdf, dqb, dkb, cdb


def _retention(q, k, v, tables):
    b, s, _ = q.shape
    c = RET_CHUNK
    n_seq = max(1, RET_STEP_ROWS // s)

    def seq(width):
        return pl.BlockSpec((n_seq, s, width), lambda bi, hi: (bi, 0, hi))

    def per_head(rows, width):
        return pl.BlockSpec((1, rows, width), lambda bi, hi: (hi, 0, 0))

    return pl.pallas_call(
        _retention_kernel,
        grid=(b // n_seq, RET_HEADS),
        in_specs=[seq(RET_QK_DIM), seq(RET_QK_DIM), seq(RET_V_DIM),
                  per_head(c, c), per_head(c, RET_QK_DIM), per_head(c, RET_QK_DIM), per_head(1, RET_V_DIM),
                  per_head(c, RET_QK_DIM), per_head(c, RET_QK_DIM), per_head(1, RET_V_DIM)],
        out_specs=seq(RET_V_DIM),
        out_shape=jax.ShapeDtypeStruct((b, s, RET_V_WIDTH), _BF16),
        compiler_params=_params(("parallel", "parallel")),
        name="retention",
    )(q, k, v, *tables)


def _rotary_tables(s):
    half = RET_QK_DIM // 2
    f32 = np.float32
    inv = np.power(f32(ROPE_BASE), -np.arange(half, dtype=f32) / f32(half)).astype(f32)
    ang = (np.arange(s, dtype=f32)[:, None] * inv[None, :]).astype(f32)
    return jnp.asarray(np.cos(ang).astype(f32)), jnp.asarray(np.sin(ang).astype(f32))


CAST_IN_CONV = ("w_up_0", "w_down_0", "w_qkvg_1")
CAST_IN_MLP0 = ("w_o_1", "w_up_1", "w_down_1")


def _trunk(x, p, tables, cast_weights):
    b, s, d = x.shape
    p = dict(p)
    names = CAST_IN_CONV if cast_weights else ()
    x, *cast = _conv_mixer(x, p["norm_mix_0"], p["w_in_conv_0"], p["conv_w_0"], p["conv_b_0"], p["w_out_conv_0"],
                           [p[n] for n in names])
    p.update(zip(names, cast))
    names = CAST_IN_MLP0 if cast_weights else ()
    x, *cast = _mlp(x.reshape(b * s, d), p["norm_mlp_0"], p["w_up_0"], p["w_down_0"], [p[n] for n in names])
    p.update(zip(names, cast))
    cos, sin = _rotary_tables(s)
    q, k, v, sg = _qkvg(x.reshape(b, s, d), p["norm_mix_1"], p["w_qkvg_1"], cos, sin)
    r = _retention(q, k, v, tables)
    y = _proj_mlp_norm(x, r.reshape(b * s, RET_V_WIDTH), sg.reshape(b * s, RET_V_WIDTH), p["w_o_1"],
                       p["norm_mlp_1"], p["w_up_1"], p["w_down_1"], p["norm_final"])
    return y.reshape(b, s, d), p


def kernel(x_prompt, x_sample, norm_mix_0, w_in_conv_0, conv_w_0, conv_b_0, w_out_conv_0, norm_mlp_0, w_up_0, w_down_0,
           norm_mix_1, w_qkvg_1, w_o_1, norm_mlp_1, w_up_1, w_down_1, norm_final):
    row = lambda t: t.reshape(1, -1)
    p = {
        "norm_mix_0": row(norm_mix_0), "w_in_conv_0": w_in_conv_0.astype(_BF16), "conv_w_0": conv_w_0,
        "conv_b_0": row(conv_b_0), "w_out_conv_0": w_out_conv_0.astype(_BF16),
        "norm_mlp_0": row(norm_mlp_0), "w_up_0": w_up_0, "w_down_0": w_down_0,
        "norm_mix_1": row(norm_mix_1), "w_qkvg_1": w_qkvg_1, "w_o_1": w_o_1,
        "norm_mlp_1": row(norm_mlp_1), "w_up_1": w_up_1, "w_down_1": w_down_1,
        "norm_final": row(norm_final),
    }
    tables = _retention_tables()
    y_prompt, p = _trunk(x_prompt, p, tables, cast_weights=True)
    y_sample, _ = _trunk(x_sample, p, tables, cast_weights=False)
    return y_prompt, y_sample
```

```python
import functools

import numpy as np
import jax
import jax.numpy as jnp
from jax import lax
from jax.experimental import pallas as pl
from jax.experimental.pallas import tpu as pltpu

D_MODEL = 1024
D_FF = 4 * D_MODEL
CONV_WIDTH = 3
RET_HEADS = 4
RET_QK_DIM = 256
RET_V_DIM = 512
RET_QK_WIDTH = RET_HEADS * RET_QK_DIM
RET_V_WIDTH = RET_HEADS * RET_V_DIM
NORM_EPS = 1e-6
ROPE_BASE = 10000.0

ROW_TILE = 512
MLP_ROW_TILE = 1024
FF_CHUNK = 1024
HALO_ROWS = 8
RET_CHUNK = 256
RET_STEP_ROWS = 4096
VMEM_LIMIT_BYTES = 56 * 1024 * 1024

_F32 = jnp.float32
_BF16 = jnp.bfloat16


def _dot(a, b):
    return jnp.dot(a, b, preferred_element_type=_F32)


def _rms_norm(x, g):
    y = x * lax.rsqrt(jnp.mean(x * x, axis=-1, keepdims=True) + NORM_EPS)
    return y * g


def _resident(shape):
    zeros = (0,) * len(shape)
    return pl.BlockSpec(shape, lambda *_: zeros, pipeline_mode=pl.Buffered(1))


def _params(semantics):
    return pltpu.CompilerParams(dimension_semantics=semantics, vmem_limit_bytes=VMEM_LIMIT_BYTES)


def _cast_specs(weights, n_steps, step_of):
    in_specs, out_specs, out_shapes = [], [], []
    for w in weights:
        rows, cols = w.shape
        block = (rows // n_steps, cols)
        in_specs.append(pl.BlockSpec(block, lambda *idx: (step_of(*idx), 0)))
        out_specs.append(pl.BlockSpec(block, lambda *idx: (step_of(*idx), 0)))
        out_shapes.append(jax.ShapeDtypeStruct(w.shape, _BF16))
    return in_specs, out_specs, out_shapes


def _cast_blocks(src_refs, dst_refs):
    for src, dst in zip(src_refs, dst_refs):
        dst[...] = src[...].astype(_BF16)


def _conv_mixer_kernel(n_cast, n_alias, x_ref, xp_ref, xn_ref, g_ref, w_in_ref, cw_ref, cb_ref, w_out_ref, *rest):
    o_ref, u_ref = rest[n_cast + n_alias], rest[-1]
    _cast_blocks(rest[:n_cast], rest[n_cast + n_alias + 1:-1])
    i = pl.program_id(1)
    n = pl.num_programs(1)
    x = x_ref[0]
    tm = x.shape[0]
    ext = tm + 2 * HALO_ROWS
    lo, hi = HALO_ROWS, HALO_ROWS + tm
    y_ext = _rms_norm(jnp.concatenate([xp_ref[0], x, xn_ref[0]], axis=0), g_ref[...])
    xe = y_ext.astype(_BF16)
    xm = y_ext[lo:hi].astype(_BF16)

    d = D_MODEL
    u_ext = _dot(xe, w_in_ref[:, 0:d]) * _dot(xe, w_in_ref[:, 2 * d:3 * d])
    u_ref[0:lo, :] = jnp.where(i > 0, u_ext[0:lo], 0.0)
    u_ref[lo:hi, :] = u_ext[lo:hi]
    u_ref[hi:ext, :] = jnp.where(i < n - 1, u_ext[hi:ext], 0.0)
    u_prev = u_ref[lo - 1:hi - 1, :]
    u_next = u_ref[lo + 1:hi + 1, :]
    z = cb_ref[...] + u_prev * cw_ref[0:1, :] + u_ext[lo:hi] * cw_ref[1:2, :] + u_next * cw_ref[2:3, :]

    gate_b = _dot(xm, w_in_ref[:, d:2 * d])
    y = _dot((gate_b * z).astype(_BF16), w_out_ref[...])
    o_ref[0] = x + y


def _conv_mixer(x, g, w_in, conv_w, conv_b, w_out, stream, tile_offset, to_cast=()):
    b, s, d = x.shape
    tm = MLP_ROW_TILE
    nt = s // tm
    hb = tm // HALO_ROWS
    last_halo = s // HALO_ROWS - 1
    cast_in, cast_out, cast_shapes = _cast_specs(to_cast, b * nt, lambda bi, i: bi * nt + i)
    in_place = not isinstance(stream, jax.ShapeDtypeStruct)
    fixed = [
        pl.BlockSpec((1, tm, d), lambda bi, i: (bi, i, 0)),
        pl.BlockSpec((1, HALO_ROWS, d), lambda bi, i: (bi, jnp.maximum(i * hb - 1, 0), 0)),
        pl.BlockSpec((1, HALO_ROWS, d), lambda bi, i: (bi, jnp.minimum((i + 1) * hb, last_halo), 0)),
        _resident((1, d)),
        _resident((d, 3 * d)),
        _resident((CONV_WIDTH, d)),
        _resident((1, d)),
        _resident((d, d)),
    ]
    alias_in = [pl.BlockSpec(memory_space=pl.ANY)] if in_place else []
    aliases = {len(fixed) + len(cast_in): 0} if in_place else {}
    return pl.pallas_call(
        functools.partial(_conv_mixer_kernel, len(to_cast), len(alias_in)),
        grid=(b, nt),
        in_specs=fixed + cast_in + alias_in,
        out_specs=[pl.BlockSpec((1, tm, d), lambda bi, i: (tile_offset + bi * nt + i, 0, 0))] + cast_out,
        out_shape=[jax.ShapeDtypeStruct(stream.shape, stream.dtype)] + cast_shapes,
        input_output_aliases=aliases,
        scratch_shapes=[pltpu.VMEM((tm + 2 * HALO_ROWS, d), _F32)],
        compiler_params=_params(("parallel", "parallel")),
        name="conv_mixer",
    )(x, x, x, g, w_in, conv_w, conv_b, w_out, *to_cast, *([stream] if in_place else []))


def _mlp_body(x, g_ref, w_up_ref, w_down_ref):
    xn = _rms_norm(x, g_ref[...]).astype(_BF16)
    acc = x
    for c in range(D_FF // FF_CHUNK):
        cols = slice(c * FF_CHUNK, (c + 1) * FF_CHUNK)
        h = jnp.square(jnp.maximum(_dot(xn, w_up_ref[:, cols]), 0.0)).astype(_BF16)
        acc = acc + _dot(h, w_down_ref[cols, :])
    return acc


def _mlp_kernel(n_cast, x_ref, g_ref, w_up_ref, w_down_ref, *rest):
    o_ref = rest[n_cast]
    _cast_blocks(rest[:n_cast], rest[n_cast + 1:])
    o_ref[...] = _mlp_body(x_ref[...], g_ref, w_up_ref, w_down_ref)


def _proj_mlp_norm_kernel(group_tiles, x_ref, r_ref, sg_ref, w_o_ref, g_ref, w_up_ref, w_down_ref, gf_ref, *o_refs):
    proj = None
    for h in range(RET_HEADS):
        cols = slice(h * RET_V_DIM, (h + 1) * RET_V_DIM)
        o = r_ref[:, cols].astype(_F32)
        o = o - jnp.mean(o, axis=-1, keepdims=True)
        o = o * lax.rsqrt(jnp.mean(o * o, axis=-1, keepdims=True) + NORM_EPS)
        gated = (sg_ref[:, cols].astype(_F32) * o).astype(_BF16)
        part = _dot(gated, w_o_ref[cols, :])
        proj = part if proj is None else proj + part
    x = x_ref[...] + proj
    y = _mlp_body(x, g_ref, w_up_ref, w_down_ref)
    t = pl.program_id(0)
    start = 0
    for n, o_ref in zip(group_tiles, o_refs):
        @pl.when(jnp.logical_and(t >= start, t < start + n))
        def _(o_ref=o_ref):
            o_ref[...] = _rms_norm(y, gf_ref[...])
        start += n


def _mlp(x, g, w_up, w_down, to_cast=()):
    t, d = x.shape
    tm = MLP_ROW_TILE
    rows = pl.BlockSpec((tm, d), lambda i: (i, 0))
    cast_in, cast_out, cast_shapes = _cast_specs(to_cast, t // tm, lambda i: i)
    return pl.pallas_call(
        functools.partial(_mlp_kernel, len(to_cast)),
        grid=(t // tm,),
        in_specs=[rows, _resident((1, d)), _resident((d, D_FF)), _resident((D_FF, d))] + cast_in,
        out_specs=[rows] + cast_out,
        out_shape=[jax.ShapeDtypeStruct((t, d), _F32)] + cast_shapes,
        compiler_params=_params(("parallel",)),
        name="mlp0",
    )(x, g, w_up, w_down, *to_cast)


def _proj_mlp_norm(stream, r, sg, group_rows, w_o, g, w_up, w_down, g_final):
    t, d = stream.shape
    tm = ROW_TILE
    group_tiles = tuple(n // tm for n in group_rows)
    rows = pl.BlockSpec((tm, d), lambda i: (i, 0))
    wide = pl.BlockSpec((tm, RET_V_WIDTH), lambda i: (i, 0))

    def group_out(start, n):
        return pl.BlockSpec((tm, d), lambda i: (jnp.clip(i - start, 0, n - 1), 0))

    starts = [sum(group_tiles[:gi]) for gi in range(len(group_tiles))]
    return pl.pallas_call(
        functools.partial(_proj_mlp_norm_kernel, group_tiles),
        grid=(t // tm,),
        in_specs=[rows, wide, wide, _resident((RET_V_WIDTH, d)),
                  _resident((1, d)), _resident((d, D_FF)), _resident((D_FF, d)), _resident((1, d))],
        out_specs=[group_out(s0, n) for s0, n in zip(starts, group_tiles)],
        out_shape=[jax.ShapeDtypeStruct((n, d), _F32) for n in group_rows],
        compiler_params=_params(("arbitrary",)),
        name="proj_mlp1_norm",
    )(stream, r, sg, w_o, g, w_up, w_down, g_final)


def _rotary(t, cos, sin):
    half = RET_QK_DIM // 2
    parts = []
    for h in range(RET_HEADS):
        x1 = t[:, h * RET_QK_DIM:h * RET_QK_DIM + half]
        x2 = t[:, h * RET_QK_DIM + half:(h + 1) * RET_QK_DIM]
        parts += [x1 * cos - x2 * sin, x1 * sin + x2 * cos]
    return jnp.concatenate(parts, axis=-1)


def _qkvg_kernel(x_ref, g_ref, w_ref, cos_ref, sin_ref, q_ref, k_ref, v_ref, sg_ref):
    xn = _rms_norm(x_ref[0], g_ref[...]).astype(_BF16)
    cos, sin = cos_ref[...], sin_ref[...]
    qw, vw = RET_QK_WIDTH, RET_V_WIDTH
    gate = _dot(xn, w_ref[:, 2 * qw + vw:2 * qw + 2 * vw])
    half_gate = 0.5 * gate
    sg_ref[0] = (half_gate * (jnp.tanh(half_gate) + 1.0)).astype(_BF16)
    q = _rotary(_dot(xn, w_ref[:, 0:qw]), cos, sin) * (RET_QK_DIM ** -0.5)
    q_ref[0] = q.astype(_BF16)
    k_ref[0] = _rotary(_dot(xn, w_ref[:, qw:2 * qw]), cos, sin).astype(_BF16)
    v_ref[0] = _dot(xn, w_ref[:, 2 * qw:2 * qw + vw]).astype(_BF16)


def _position_tile(t, group_tiles, seq_tiles):
    pos, start = None, 0
    for n, per_seq in zip(group_tiles, seq_tiles):
        here = (t - start) % per_seq
        pos = here if pos is None else jnp.where(t >= start, here, pos)
        start += n
    return pos


def _qkvg(stream, group_tiles, seq_tiles, g, w, cos, sin):
    n_tiles, tm, d = stream.shape
    half = RET_QK_DIM // 2

    def rows(width):
        return pl.BlockSpec((1, tm, width), lambda t: (t, 0, 0))

    def out(width):
        return jax.ShapeDtypeStruct((n_tiles, tm, width), _BF16)

    table = pl.BlockSpec((tm, half), lambda t: (_position_tile(t, group_tiles, seq_tiles), 0))
    return pl.pallas_call(
        _qkvg_kernel,
        grid=(n_tiles,),
        in_specs=[rows(d), _resident((1, d)), _resident((d, 2 * RET_QK_WIDTH + 2 * RET_V_WIDTH)), table, table],
        out_specs=[rows(RET_QK_WIDTH), rows(RET_QK_WIDTH), rows(RET_V_WIDTH), rows(RET_V_WIDTH)],
        out_shape=[out(RET_QK_WIDTH), out(RET_QK_WIDTH), out(RET_V_WIDTH), out(RET_V_WIDTH)],
        compiler_params=_params(("parallel",)),
        name="qkvg_rotary",
    )(stream, g, w, cos, sin)


def _retention_kernel(n_alias, q_ref, k_ref, v_ref, dmat_ref, dqf_ref, dkf_ref, cdf_ref, dqb_ref, dkb_ref, cdb_ref,
                      *rest):
    o_ref = rest[n_alias]
    c = RET_CHUNK
    n_seq, seq_len, _ = q_ref.shape
    n_chunks = seq_len // c
    contract_rows = (((0,), (0,)), ((), ()))
    contract_cols = (((1,), (1,)), ((), ()))

    def chunk(j):
        return slice(j * c, (j + 1) * c)

    def decayed_kv(b, rows, dk_ref):
        kd = (k_ref[b, rows, :].astype(_F32) * dk_ref[0]).astype(_BF16)
        return lax.dot_general(kd, v_ref[b, rows, :], contract_rows, preferred_element_type=_F32)

    def emit(b, j, earlier, later):
        rows = chunk(j)
        qc = q_ref[b, rows, :]
        scores = lax.dot_general(qc, k_ref[b, rows, :], contract_cols, preferred_element_type=_F32) * dmat_ref[0]
        o = _dot(scores.astype(_BF16), v_ref[b, rows, :])
        qf = qc.astype(_F32)
        scaled, states = [], []
        if earlier is not None:
            scaled.append((qf * dqf_ref[0]).astype(_BF16))
            states.append(earlier)
        if later is not None:
            scaled.append((qf * dqb_ref[0]).astype(_BF16))
            states.append(later)
        if scaled:
            o = o + _dot(jnp.concatenate(scaled, axis=1), jnp.concatenate(states, axis=0))
        o_ref[b, rows, :] = o.astype(o_ref.dtype)

    for b in range(n_seq):
        earlier = [None] * n_chunks
        later = [None] * n_chunks
        sf = sb = None
        for t in range(n_chunks - 1):
            kv = decayed_kv(b, chunk(t), dkf_ref)
            sf = kv if sf is None else sf * cdf_ref[0] + kv
            earlier[t + 1] = sf.astype(_BF16)
            j = n_chunks - 1 - t
            kv = decayed_kv(b, chunk(j), dkb_ref)
            sb = kv if sb is None else sb * cdb_ref[0] + kv
            later[j - 1] = sb.astype(_BF16)
        for j in range(n_chunks):
            emit(b, j, earlier[j], later[j])


def _retention_tables():
    c = RET_CHUNK
    f32 = np.float32
    h = np.arange(RET_HEADS, dtype=f32)
    lg_f = np.log(f32(1.0) - np.power(f32(2.0), f32(-5.0) - h)).astype(f32)[:, None, None]
    lg_b = np.log(f32(1.0) - np.power(f32(2.0), f32(-5.5) - h)).astype(f32)[:, None, None]
    idx = np.arange(c, dtype=f32)
    diff = idx[:, None] - idx[None, :]
    dmat = np.where(diff >= 0, np.exp(lg_f * np.maximum(diff, f32(0.0))), np.exp(lg_b * np.maximum(-diff, f32(0.0))))
    col = idx[None, :, None]

    def wide(t, width):
        return jnp.asarray(np.ascontiguousarray(np.broadcast_to(t.astype(f32), (RET_HEADS, t.shape[1], width))))

    dqf = wide(np.exp(lg_f * (col + f32(1.0))), RET_QK_DIM)
    dkf = wide(np.exp(lg_f * (f32(c - 1.0) - col)), RET_QK_DIM)
    cdf = wide(np.exp(lg_f * f32(c)), RET_V_DIM)
    dqb = wide(np.exp(lg_b * (f32(c) - col)), RET_QK_DIM)
    dkb = wide(np.exp(lg_b * col), RET_QK_DIM)
    cdb = wide(np.exp(lg_b * f32(c)), RET_V_DIM)
    return jnp.asarray(dmat.astype(f32)), dqf, dkf, cdf, dqb, dkb, cdb


def _retention(q, k, v, tables, row_offset, b, s, out):
    c = RET_CHUNK
    n_seq = max(1, RET_STEP_ROWS // s)
    first = row_offset // (n_seq * s)
    in_place = not isinstance(out, jax.ShapeDtypeStruct)

    def seq(width):
        return pl.BlockSpec((n_seq, s, width), lambda bi, hi: (first + bi, 0, hi))

    def per_head(rows, width):
        return pl.BlockSpec((1, rows, width), lambda bi, hi: (hi, 0, 0))

    def sequences(t):
        return t.reshape(-1, s, t.shape[-1])

    in_specs = [seq(RET_QK_DIM), seq(RET_QK_DIM), seq(RET_V_DIM),
                per_head(c, c), per_head(c, RET_QK_DIM), per_head(c, RET_QK_DIM), per_head(1, RET_V_DIM),
                per_head(c, RET_QK_DIM), per_head(c, RET_QK_DIM), per_head(1, RET_V_DIM)]
    alias_in = [pl.BlockSpec(memory_space=pl.ANY)] if in_place else []
    r = pl.pallas_call(
        functools.partial(_retention_kernel, len(alias_in)),
        grid=(b // n_seq, RET_HEADS),
        in_specs=in_specs + alias_in,
        out_specs=seq(RET_V_DIM),
        out_shape=jax.ShapeDtypeStruct((out.shape[0] // s, s, RET_V_WIDTH), out.dtype),
        input_output_aliases={len(in_specs): 0} if in_place else {},
        compiler_params=_params(("parallel", "parallel")),
        name="retention",
    )(sequences(q), sequences(k), sequences(v), *tables, *([sequences(out)] if in_place else []))
    return r.reshape(out.shape)


def _rotary_tables(s):
    half = RET_QK_DIM // 2
    f32 = np.float32
    inv = np.power(f32(ROPE_BASE), -np.arange(half, dtype=f32) / f32(half)).astype(f32)
    ang = (np.arange(s, dtype=f32)[:, None] * inv[None, :]).astype(f32)
    return jnp.asarray(np.cos(ang).astype(f32)), jnp.asarray(np.sin(ang).astype(f32))


CAST_IN_CONV = ("w_up_0", "w_down_0", "w_qkvg_1")
CAST_IN_MLP0 = ("w_o_1", "w_up_1", "w_down_1")


def _trunks(groups, p):
    p = dict(p)
    d = groups[0].shape[-1]
    tm = MLP_ROW_TILE
    tiles = [b * s // tm for b, s, _ in (x.shape for x in groups)]
    stream = jax.ShapeDtypeStruct((sum(tiles), tm, d), _F32)
    for gi, x in enumerate(groups):
        names = CAST_IN_CONV if gi == 0 else ()
        stream, *cast = _conv_mixer(x, p["norm_mix_0"], p["w_in_conv_0"], p["conv_w_0"], p["conv_b_0"],
                                    p["w_out_conv_0"], stream, sum(tiles[:gi]), [p[n] for n in names])
        p.update(zip(names, cast))
    stream, *cast = _mlp(stream.reshape(-1, d), p["norm_mlp_0"], p["w_up_0"], p["w_down_0"],
                         [p[n] for n in CAST_IN_MLP0])
    p.update(zip(CAST_IN_MLP0, cast))

    cos, sin = _rotary_tables(max(x.shape[1] for x in groups))
    q, k, v, sg = (t.reshape(-1, t.shape[-1]) for t in
                   _qkvg(stream.reshape(-1, tm, d), tiles, [x.shape[1] // tm for x in groups],
                         p["norm_mix_1"], p["w_qkvg_1"], cos, sin))
    tables = _retention_tables()
    r = jax.ShapeDtypeStruct((sum(tiles) * tm, RET_V_WIDTH), _BF16)
    for gi, x in enumerate(groups):
        r = _retention(q, k, v, tables, sum(tiles[:gi]) * tm, x.shape[0], x.shape[1], r)
    ys = _proj_mlp_norm(stream, r, sg, [n * tm for n in tiles], p["w_o_1"], p["norm_mlp_1"], p["w_up_1"],
                        p["w_down_1"], p["norm_final"])
    return tuple(y.reshape(x.shape) for y, x in zip(ys, groups))


def kernel(x_prompt, x_sample, norm_mix_0, w_in_conv_0, conv_w_0, conv_b_0, w_out_conv_0, norm_mlp_0, w_up_0, w_down_0,
           norm_mix_1, w_qkvg_1, w_o_1, norm_mlp_1, w_up_1, w_down_1, norm_final):
    row = lambda t: t.reshape(1, -1)
    p = {
        "norm_mix_0": row(norm_mix_0), "w_in_conv_0": w_in_conv_0.astype(_BF16), "conv_w_0": conv_w_0,
        "conv_b_0": row(conv_b_0), "w_out_conv_0": w_out_conv_0.astype(_BF16),
        "norm_mlp_0": row(norm_mlp_0), "w_up_0": w_up_0, "w_down_0": w_down_0,
        "norm_mix_1": row(norm_mix_1), "w_qkvg_1": w_qkvg_1, "w_o_1": w_o_1,
        "norm_mlp_1": row(norm_mlp_1), "w_up_1": w_up_1, "w_down_1": w_down_1,
        "norm_final": row(norm_final),
    }
    return _trunks((x_prompt, x_sample), p)
```

```python
import functools

import numpy as np
import jax
import jax.numpy as jnp
from jax import lax
from jax.experimental import pallas as pl
from jax.experimental.pallas import tpu as pltpu

D_MODEL = 1024
D_FF = 4 * D_MODEL
CONV_WIDTH = 3
RET_HEADS = 4
RET_QK_DIM = 256
RET_V_DIM = 512
RET_QK_WIDTH = RET_HEADS * RET_QK_DIM
RET_V_WIDTH = RET_HEADS * RET_V_DIM
NORM_EPS = 1e-6
ROPE_BASE = 10000.0

ROW_TILE = 512
MLP_ROW_TILE = 1024
FF_CHUNK = 1024
HALO_ROWS = 8
RET_CHUNK = 256
RET_STEP_ROWS = 4096
VMEM_LIMIT_BYTES = 56 * 1024 * 1024

_F32 = jnp.float32
_BF16 = jnp.bfloat16


def _dot(a, b):
    return jnp.dot(a, b, preferred_element_type=_F32)


def _rms_norm(x, g):
    y = x * lax.rsqrt(jnp.mean(x * x, axis=-1, keepdims=True) + NORM_EPS)
    return y * g


def _resident(shape):
    zeros = (0,) * len(shape)
    return pl.BlockSpec(shape, lambda *_: zeros, pipeline_mode=pl.Buffered(1))


def _params(semantics):
    return pltpu.CompilerParams(dimension_semantics=semantics, vmem_limit_bytes=VMEM_LIMIT_BYTES)


def _cast_specs(weights, n_steps, step_of):
    in_specs, out_specs, out_shapes = [], [], []
    for w in weights:
        rows, cols = w.shape
        block = (rows // n_steps, cols)
        in_specs.append(pl.BlockSpec(block, lambda *idx: (step_of(*idx), 0)))
        out_specs.append(pl.BlockSpec(block, lambda *idx: (step_of(*idx), 0)))
        out_shapes.append(jax.ShapeDtypeStruct(w.shape, _BF16))
    return in_specs, out_specs, out_shapes


def _cast_blocks(src_refs, dst_refs):
    for src, dst in zip(src_refs, dst_refs):
        dst[...] = src[...].astype(_BF16)


def _conv_mixer_kernel(n_cast, n_alias, x_ref, xp_ref, xn_ref, g_ref, w_in_ref, cw_ref, cb_ref, w_out_ref, *rest):
    o_ref, u_ref = rest[n_cast + n_alias], rest[-1]
    _cast_blocks(rest[:n_cast], rest[n_cast + n_alias + 1:-1])
    i = pl.program_id(1)
    n = pl.num_programs(1)
    x = x_ref[0]
    tm = x.shape[0]
    ext = tm + 2 * HALO_ROWS
    lo, hi = HALO_ROWS, HALO_ROWS + tm
    y_ext = _rms_norm(jnp.concatenate([xp_ref[0], x, xn_ref[0]], axis=0), g_ref[...])
    xe = y_ext.astype(_BF16)
    xm = y_ext[lo:hi].astype(_BF16)

    d = D_MODEL
    u_ext = _dot(xe, w_in_ref[:, 0:d]) * _dot(xe, w_in_ref[:, 2 * d:3 * d])
    u_ref[0:lo, :] = jnp.where(i > 0, u_ext[0:lo], 0.0)
    u_ref[lo:hi, :] = u_ext[lo:hi]
    u_ref[hi:ext, :] = jnp.where(i < n - 1, u_ext[hi:ext], 0.0)
    u_prev = u_ref[lo - 1:hi - 1, :]
    u_next = u_ref[lo + 1:hi + 1, :]
    z = cb_ref[...] + u_prev * cw_ref[0:1, :] + u_ext[lo:hi] * cw_ref[1:2, :] + u_next * cw_ref[2:3, :]

    gate_b = _dot(xm, w_in_ref[:, d:2 * d])
    y = _dot((gate_b * z).astype(_BF16), w_out_ref[...])
    o_ref[0] = x + y


def _conv_mixer(x, g, w_in, conv_w, conv_b, w_out, stream, tile_offset, to_cast=()):
    b, s, d = x.shape
    tm = MLP_ROW_TILE
    nt = s // tm
    hb = tm // HALO_ROWS
    last_halo = s // HALO_ROWS - 1
    cast_in, cast_out, cast_shapes = _cast_specs(to_cast, b * nt, lambda bi, i: bi * nt + i)
    in_place = not isinstance(stream, jax.ShapeDtypeStruct)
    fixed = [
        pl.BlockSpec((1, tm, d), lambda bi, i: (bi, i, 0)),
        pl.BlockSpec((1, HALO_ROWS, d), lambda bi, i: (bi, jnp.maximum(i * hb - 1, 0), 0)),
        pl.BlockSpec((1, HALO_ROWS, d), lambda bi, i: (bi, jnp.minimum((i + 1) * hb, last_halo), 0)),
        _resident((1, d)),
        _resident((d, 3 * d)),
        _resident((CONV_WIDTH, d)),
        _resident((1, d)),
        _resident((d, d)),
    ]
    alias_in = [pl.BlockSpec(memory_space=pl.ANY)] if in_place else []
    aliases = {len(fixed) + len(cast_in): 0} if in_place else {}
    return pl.pallas_call(
        functools.partial(_conv_mixer_kernel, len(to_cast), len(alias_in)),
        grid=(b, nt),
        in_specs=fixed + cast_in + alias_in,
        out_specs=[pl.BlockSpec((1, tm, d), lambda bi, i: (tile_offset + bi * nt + i, 0, 0))] + cast_out,
        out_shape=[jax.ShapeDtypeStruct(stream.shape, stream.dtype)] + cast_shapes,
        input_output_aliases=aliases,
        scratch_shapes=[pltpu.VMEM((tm + 2 * HALO_ROWS, d), _F32)],
        compiler_params=_params(("parallel", "parallel")),
        name="conv_mixer",
    )(x, x, x, g, w_in, conv_w, conv_b, w_out, *to_cast, *([stream] if in_place else []))


def _mlp_body(x, g_ref, w_up_ref, w_down_ref):
    xn = _rms_norm(x, g_ref[...]).astype(_BF16)
    acc = x
    for c in range(D_FF // FF_CHUNK):
        cols = slice(c * FF_CHUNK, (c + 1) * FF_CHUNK)
        h = jnp.square(jnp.maximum(_dot(xn, w_up_ref[:, cols]), 0.0)).astype(_BF16)
        acc = acc + _dot(h, w_down_ref[cols, :])
    return acc


def _mlp_kernel(n_cast, x_ref, g_ref, w_up_ref, w_down_ref, *rest):
    o_ref = rest[n_cast]
    _cast_blocks(rest[:n_cast], rest[n_cast + 1:])
    o_ref[...] = _mlp_body(x_ref[...], g_ref, w_up_ref, w_down_ref)


def _proj_mlp_norm_kernel(group_tiles, x_ref, r_ref, sg_ref, w_o_ref, g_ref, w_up_ref, w_down_ref, gf_ref, *o_refs):
    def tile_result():
        proj = None
        for h in range(RET_HEADS):
            cols = slice(h * RET_V_DIM, (h + 1) * RET_V_DIM)
            o = r_ref[:, cols].astype(_F32)
            o = o - jnp.mean(o, axis=-1, keepdims=True)
            o = o * lax.rsqrt(jnp.mean(o * o, axis=-1, keepdims=True) + NORM_EPS)
            gated = (sg_ref[:, cols].astype(_F32) * o).astype(_BF16)
            part = _dot(gated, w_o_ref[cols, :])
            proj = part if proj is None else proj + part
        x = x_ref[...] + proj
        return _rms_norm(_mlp_body(x, g_ref, w_up_ref, w_down_ref), gf_ref[...])

    t = pl.program_id(0)
    start = 0
    for n, o_ref in zip(group_tiles, o_refs):
        @pl.when(jnp.logical_and(t >= start, t < start + n))
        def _(o_ref=o_ref):
            o_ref[...] = tile_result()
        start += n


def _mlp(x, g, w_up, w_down, to_cast=()):
    t, d = x.shape
    tm = MLP_ROW_TILE
    rows = pl.BlockSpec((tm, d), lambda i: (i, 0))
    cast_in, cast_out, cast_shapes = _cast_specs(to_cast, t // tm, lambda i: i)
    return pl.pallas_call(
        functools.partial(_mlp_kernel, len(to_cast)),
        grid=(t // tm,),
        in_specs=[rows, _resident((1, d)), _resident((d, D_FF)), _resident((D_FF, d))] + cast_in,
        out_specs=[rows] + cast_out,
        out_shape=[jax.ShapeDtypeStruct((t, d), _F32)] + cast_shapes,
        compiler_params=_params(("parallel",)),
        name="mlp0",
    )(x, g, w_up, w_down, *to_cast)


def _proj_mlp_norm(stream, r, sg, group_rows, w_o, g, w_up, w_down, g_final):
    t, d = stream.shape
    tm = ROW_TILE
    group_tiles = tuple(n // tm for n in group_rows)
    rows = pl.BlockSpec((tm, d), lambda i: (i, 0))
    wide = pl.BlockSpec((tm, RET_V_WIDTH), lambda i: (i, 0))

    def group_out(start, n):
        return pl.BlockSpec((tm, d), lambda i: (jnp.clip(i - start, 0, n - 1), 0))

    starts = [sum(group_tiles[:gi]) for gi in range(len(group_tiles))]
    return pl.pallas_call(
        functools.partial(_proj_mlp_norm_kernel, group_tiles),
        grid=(t // tm,),
        in_specs=[rows, wide, wide, _resident((RET_V_WIDTH, d)),
                  _resident((1, d)), _resident((d, D_FF)), _resident((D_FF, d)), _resident((1, d))],
        out_specs=[group_out(s0, n) for s0, n in zip(starts, group_tiles)],
        out_shape=[jax.ShapeDtypeStruct((n, d), _F32) for n in group_rows],
        compiler_params=_params(("arbitrary",)),
        name="proj_mlp1_norm",
    )(stream, r, sg, w_o, g, w_up, w_down, g_final)


def _rotary(t, cos, sin):
    half = RET_QK_DIM // 2
    parts = []
    for h in range(RET_HEADS):
        x1 = t[:, h * RET_QK_DIM:h * RET_QK_DIM + half]
        x2 = t[:, h * RET_QK_DIM + half:(h + 1) * RET_QK_DIM]
        parts += [x1 * cos - x2 * sin, x1 * sin + x2 * cos]
    return jnp.concatenate(parts, axis=-1)


def _qkvg_kernel(x_ref, g_ref, w_ref, cos_ref, sin_ref, q_ref, k_ref, v_ref, sg_ref):
    xn = _rms_norm(x_ref[0], g_ref[...]).astype(_BF16)
    cos, sin = cos_ref[...], sin_ref[...]
    qw, vw = RET_QK_WIDTH, RET_V_WIDTH
    gate = _dot(xn, w_ref[:, 2 * qw + vw:2 * qw + 2 * vw])
    half_gate = 0.5 * gate
    sg_ref[0] = (half_gate * (jnp.tanh(half_gate) + 1.0)).astype(_BF16)
    q = _rotary(_dot(xn, w_ref[:, 0:qw]), cos, sin) * (RET_QK_DIM ** -0.5)
    q_ref[0] = q.astype(_BF16)
    k_ref[0] = _rotary(_dot(xn, w_ref[:, qw:2 * qw]), cos, sin).astype(_BF16)
    v_ref[0] = _dot(xn, w_ref[:, 2 * qw:2 * qw + vw]).astype(_BF16)


def _position_tile(t, group_tiles, seq_tiles):
    pos, start = None, 0
    for n, per_seq in zip(group_tiles, seq_tiles):
        here = (t - start) % per_seq
        pos = here if pos is None else jnp.where(t >= start, here, pos)
        start += n
    return pos


def _qkvg(stream, group_tiles, seq_tiles, g, w, cos, sin):
    n_tiles, tm, d = stream.shape
    half = RET_QK_DIM // 2

    def rows(width):
        return pl.BlockSpec((1, tm, width), lambda t: (t, 0, 0))

    def out(width):
        return jax.ShapeDtypeStruct((n_tiles, tm, width), _BF16)

    table = pl.BlockSpec((tm, half), lambda t: (_position_tile(t, group_tiles, seq_tiles), 0))
    return pl.pallas_call(
        _qkvg_kernel,
        grid=(n_tiles,),
        in_specs=[rows(d), _resident((1, d)), _resident((d, 2 * RET_QK_WIDTH + 2 * RET_V_WIDTH)), table, table],
        out_specs=[rows(RET_QK_WIDTH), rows(RET_QK_WIDTH), rows(RET_V_WIDTH), rows(RET_V_WIDTH)],
        out_shape=[out(RET_QK_WIDTH), out(RET_QK_WIDTH), out(RET_V_WIDTH), out(RET_V_WIDTH)],
        compiler_params=_params(("parallel",)),
        name="qkvg_rotary",
    )(stream, g, w, cos, sin)


def _retention_kernel(group_blocks, group_seq, q_ref, k_ref, v_ref, dmat_ref, dqf_ref, dkf_ref, cdf_ref,
                      dqb_ref, dkb_ref, cdb_ref, o_ref):
    c = RET_CHUNK
    block_rows = q_ref.shape[1]
    contract_rows = (((0,), (0,)), ((), ()))
    contract_cols = (((1,), (1,)), ((), ()))

    def decayed_kv(rows, dk_ref):
        kd = (k_ref[0, rows, :].astype(_F32) * dk_ref[0]).astype(_BF16)
        return lax.dot_general(kd, v_ref[0, rows, :], contract_rows, preferred_element_type=_F32)

    def emit(rows, earlier, later):
        qc = q_ref[0, rows, :]
        scores = lax.dot_general(qc, k_ref[0, rows, :], contract_cols, preferred_element_type=_F32) * dmat_ref[0]
        o = _dot(scores.astype(_BF16), v_ref[0, rows, :])
        qf = qc.astype(_F32)
        scaled, states = [], []
        if earlier is not None:
            scaled.append((qf * dqf_ref[0]).astype(_BF16))
            states.append(earlier)
        if later is not None:
            scaled.append((qf * dqb_ref[0]).astype(_BF16))
            states.append(later)
        if scaled:
            o = o + _dot(jnp.concatenate(scaled, axis=1), jnp.concatenate(states, axis=0))
        o_ref[0, rows, :] = o.astype(o_ref.dtype)

    def sequences(seq_len):
        n_chunks = seq_len // c
        for first_row in range(0, block_rows, seq_len):
            def chunk(j):
                return slice(first_row + j * c, first_row + (j + 1) * c)

            earlier = [None] * n_chunks
            later = [None] * n_chunks
            sf = sb = None
            for t in range(n_chunks - 1):
                kv = decayed_kv(chunk(t), dkf_ref)
                sf = kv if sf is None else sf * cdf_ref[0] + kv
                earlier[t + 1] = sf.astype(_BF16)
                j = n_chunks - 1 - t
                kv = decayed_kv(chunk(j), dkb_ref)
                sb = kv if sb is None else sb * cdb_ref[0] + kv
                later[j - 1] = sb.astype(_BF16)
            for j in range(n_chunks):
                emit(chunk(j), earlier[j], later[j])

    block = pl.program_id(0)
    start = 0
    for n, seq_len in zip(group_blocks, group_seq):
        pl.when(jnp.logical_and(block >= start, block < start + n))(functools.partial(sequences, seq_len))
        start += n


def _retention_tables():
    c = RET_CHUNK
    f32 = np.float32
    h = np.arange(RET_HEADS, dtype=f32)
    lg_f = np.log(f32(1.0) - np.power(f32(2.0), f32(-5.0) - h)).astype(f32)[:, None, None]
    lg_b = np.log(f32(1.0) - np.power(f32(2.0), f32(-5.5) - h)).astype(f32)[:, None, None]
    idx = np.arange(c, dtype=f32)
    diff = idx[:, None] - idx[None, :]
    dmat = np.where(diff >= 0, np.exp(lg_f * np.maximum(diff, f32(0.0))), np.exp(lg_b * np.maximum(-diff, f32(0.0))))
    col = idx[None, :, None]

    def wide(t, width):
        return jnp.asarray(np.ascontiguousarray(np.broadcast_to(t.astype(f32), (RET_HEADS, t.shape[1], width))))

    dqf = wide(np.exp(lg_f * (col + f32(1.0))), RET_QK_DIM)
    dkf = wide(np.exp(lg_f * (f32(c - 1.0) - col)), RET_QK_DIM)
    cdf = wide(np.exp(lg_f * f32(c)), RET_V_DIM)
    dqb = wide(np.exp(lg_b * (f32(c) - col)), RET_QK_DIM)
    dkb = wide(np.exp(lg_b * col), RET_QK_DIM)
    cdb = wide(np.exp(lg_b * f32(c)), RET_V_DIM)
    return jnp.asarray(dmat.astype(f32)), dqf, dkf, cdf, dqb, dkb, cdb


def _retention(q, k, v, tables, group_rows, group_seq):
    c = RET_CHUNK
    rows = q.shape[0]
    step = RET_STEP_ROWS
    assert all(n % step == 0 and step % s == 0 for n, s in zip(group_rows, group_seq))

    def blocks(width):
        return pl.BlockSpec((1, step, width), lambda bi, hi: (bi, 0, hi))

    def per_head(n, width):
        return pl.BlockSpec((1, n, width), lambda bi, hi: (hi, 0, 0))

    def blocked(t):
        return t.reshape(-1, step, t.shape[-1])

    r = pl.pallas_call(
        functools.partial(_retention_kernel, tuple(n // step for n in group_rows), tuple(group_seq)),
        grid=(rows // step, RET_HEADS),
        in_specs=[blocks(RET_QK_DIM), blocks(RET_QK_DIM), blocks(RET_V_DIM),
                  per_head(c, c), per_head(c, RET_QK_DIM), per_head(c, RET_QK_DIM), per_head(1, RET_V_DIM),
                  per_head(c, RET_QK_DIM), per_head(c, RET_QK_DIM), per_head(1, RET_V_DIM)],
        out_specs=blocks(RET_V_DIM),
        out_shape=jax.ShapeDtypeStruct((rows // step, step, RET_V_WIDTH), _BF16),
        compiler_params=_params(("parallel", "parallel")),
        name="retention",
    )(blocked(q), blocked(k), blocked(v), *tables)
    return r.reshape(rows, RET_V_WIDTH)


def _rotary_tables(s):
    half = RET_QK_DIM // 2
    f32 = np.float32
    inv = np.power(f32(ROPE_BASE), -np.arange(half, dtype=f32) / f32(half)).astype(f32)
    ang = (np.arange(s, dtype=f32)[:, None] * inv[None, :]).astype(f32)
    return jnp.asarray(np.cos(ang).astype(f32)), jnp.asarray(np.sin(ang).astype(f32))


CAST_IN_CONV = ("w_up_0", "w_down_0", "w_qkvg_1")
CAST_IN_MLP0 = ("w_o_1", "w_up_1", "w_down_1")


def _trunks(groups, p):
    p = dict(p)
    d = groups[0].shape[-1]
    tm = MLP_ROW_TILE
    tiles = [b * s // tm for b, s, _ in (x.shape for x in groups)]
    stream = jax.ShapeDtypeStruct((sum(tiles), tm, d), _F32)
    for gi, x in enumerate(groups):
        names = CAST_IN_CONV if gi == 0 else ()
        stream, *cast = _conv_mixer(x, p["norm_mix_0"], p["w_in_conv_0"], p["conv_w_0"], p["conv_b_0"],
                                    p["w_out_conv_0"], stream, sum(tiles[:gi]), [p[n] for n in names])
        p.update(zip(names, cast))
    stream, *cast = _mlp(stream.reshape(-1, d), p["norm_mlp_0"], p["w_up_0"], p["w_down_0"],
                         [p[n] for n in CAST_IN_MLP0])
    p.update(zip(CAST_IN_MLP0, cast))

    cos, sin = _rotary_tables(max(x.shape[1] for x in groups))
    q, k, v, sg = (t.reshape(-1, t.shape[-1]) for t in
                   _qkvg(stream.reshape(-1, tm, d), tiles, [x.shape[1] // tm for x in groups],
                         p["norm_mix_1"], p["w_qkvg_1"], cos, sin))
    r = _retention(q, k, v, _retention_tables(), [n * tm for n in tiles], [x.shape[1] for x in groups])
    ys = _proj_mlp_norm(stream, r, sg, [n * tm for n in tiles], p["w_o_1"], p["norm_mlp_1"], p["w_up_1"],
                        p["w_down_1"], p["norm_final"])
    return tuple(y.reshape(x.shape) for y, x in zip(ys, groups))


def kernel(x_prompt, x_sample, norm_mix_0, w_in_conv_0, conv_w_0, conv_b_0, w_out_conv_0, norm_mlp_0, w_up_0, w_down_0,
           norm_mix_1, w_qkvg_1, w_o_1, norm_mlp_1, w_up_1, w_down_1, norm_final):
    row = lambda t: t.reshape(1, -1)
    p = {
        "norm_mix_0": row(norm_mix_0), "w_in_conv_0": w_in_conv_0.astype(_BF16), "conv_w_0": conv_w_0,
        "conv_b_0": row(conv_b_0), "w_out_conv_0": w_out_conv_0.astype(_BF16),
        "norm_mlp_0": row(norm_mlp_0), "w_up_0": w_up_0, "w_down_0": w_down_0,
        "norm_mix_1": row(norm_mix_1), "w_qkvg_1": w_qkvg_1, "w_o_1": w_o_1,
        "norm_mlp_1": row(norm_mlp_1), "w_up_1": w_up_1, "w_down_1": w_down_1,
        "norm_final": row(norm_final),
    }
    return _trunks((x_prompt, x_sample), p)
```

```python
import functools

import numpy as np
import jax
import jax.numpy as jnp
from jax import lax
from jax.experimental import pallas as pl
from jax.experimental.pallas import tpu as pltpu

D_MODEL = 1024
D_FF = 4 * D_MODEL
CONV_WIDTH = 3
RET_HEADS = 4
RET_QK_DIM = 256
RET_V_DIM = 512
RET_QK_WIDTH = RET_HEADS * RET_QK_DIM
RET_V_WIDTH = RET_HEADS * RET_V_DIM
NORM_EPS = 1e-6
ROPE_BASE = 10000.0

ROW_TILE = 512
MLP_ROW_TILE = 1024
FF_CHUNK = 1024
HALO_ROWS = 8
RET_CHUNK = 256
RET_STEP_ROWS = 4096
VMEM_LIMIT_BYTES = 56 * 1024 * 1024

_F32 = jnp.float32
_BF16 = jnp.bfloat16


def _dot(a, b):
    return jnp.dot(a, b, preferred_element_type=_F32)


def _rms_norm(x, g):
    y = x * lax.rsqrt(jnp.mean(x * x, axis=-1, keepdims=True) + NORM_EPS)
    return y * g


def _resident(shape):
    zeros = (0,) * len(shape)
    return pl.BlockSpec(shape, lambda *_: zeros, pipeline_mode=pl.Buffered(1))


def _params(semantics):
    return pltpu.CompilerParams(dimension_semantics=semantics, vmem_limit_bytes=VMEM_LIMIT_BYTES)


def _cast_specs(weights, n_steps, step_of):
    in_specs, out_specs, out_shapes = [], [], []
    for w in weights:
        rows, cols = w.shape
        block = (rows // n_steps, cols)
        in_specs.append(pl.BlockSpec(block, lambda *idx: (step_of(*idx), 0)))
        out_specs.append(pl.BlockSpec(block, lambda *idx: (step_of(*idx), 0)))
        out_shapes.append(jax.ShapeDtypeStruct(w.shape, _BF16))
    return in_specs, out_specs, out_shapes


def _cast_blocks(src_refs, dst_refs):
    for src, dst in zip(src_refs, dst_refs):
        dst[...] = src[...].astype(_BF16)


def _conv_mixer_kernel(group_tiles, seq_tiles, n_cast, *refs):
    n_groups = len(group_tiles)
    x_refs = [refs[3 * gi:3 * gi + 3] for gi in range(n_groups)]
    g_ref, w_in_ref, cw_ref, cb_ref, w_out_ref = refs[3 * n_groups:3 * n_groups + 5]
    rest = refs[3 * n_groups + 5:]
    o_ref, u_ref = rest[n_cast], rest[-1]
    _cast_blocks(rest[:n_cast], rest[n_cast + 1:-1])

    def mix(x_ref, xp_ref, xn_ref, i, n):
        x = x_ref[0]
        tm = x.shape[0]
        ext = tm + 2 * HALO_ROWS
        lo, hi = HALO_ROWS, HALO_ROWS + tm
        y_ext = _rms_norm(jnp.concatenate([xp_ref[0], x, xn_ref[0]], axis=0), g_ref[...])
        xe = y_ext.astype(_BF16)
        xm = y_ext[lo:hi].astype(_BF16)

        d = D_MODEL
        u_ext = _dot(xe, w_in_ref[:, 0:d]) * _dot(xe, w_in_ref[:, 2 * d:3 * d])
        u_ref[0:lo, :] = jnp.where(i > 0, u_ext[0:lo], 0.0)
        u_ref[lo:hi, :] = u_ext[lo:hi]
        u_ref[hi:ext, :] = jnp.where(i < n - 1, u_ext[hi:ext], 0.0)
        u_prev = u_ref[lo - 1:hi - 1, :]
        u_next = u_ref[lo + 1:hi + 1, :]
        z = cb_ref[...] + u_prev * cw_ref[0:1, :] + u_ext[lo:hi] * cw_ref[1:2, :] + u_next * cw_ref[2:3, :]

        gate_b = _dot(xm, w_in_ref[:, d:2 * d])
        y = _dot((gate_b * z).astype(_BF16), w_out_ref[...])
        o_ref[0] = x + y

    t = pl.program_id(0)
    start = 0
    for (x_ref, xp_ref, xn_ref), n, per_seq in zip(x_refs, group_tiles, seq_tiles):
        @pl.when(jnp.logical_and(t >= start, t < start + n))
        def _(x_ref=x_ref, xp_ref=xp_ref, xn_ref=xn_ref, start=start, per_seq=per_seq):
            mix(x_ref, xp_ref, xn_ref, (t - start) % per_seq, per_seq)
        start += n


def _conv_mixer(groups, g, w_in, conv_w, conv_b, w_out, to_cast=()):
    d = groups[0].shape[-1]
    tm = MLP_ROW_TILE
    hb = tm // HALO_ROWS
    group_tiles = [x.shape[0] * x.shape[1] // tm for x in groups]
    seq_tiles = [x.shape[1] // tm for x in groups]
    n_tiles = sum(group_tiles)

    def group_specs(start, n, per_seq):
        last_halo = per_seq * hb - 1

        def where(t):
            local = jnp.clip(t - start, 0, n - 1)
            return local // per_seq, local % per_seq

        def main(t):
            bi, i = where(t)
            return bi, i, 0

        def before(t):
            bi, i = where(t)
            return bi, jnp.maximum(i * hb - 1, 0), 0

        def after(t):
            bi, i = where(t)
            return bi, jnp.minimum((i + 1) * hb, last_halo), 0

        return [pl.BlockSpec((1, tm, d), main), pl.BlockSpec((1, HALO_ROWS, d), before),
                pl.BlockSpec((1, HALO_ROWS, d), after)]

    x_specs, start = [], 0
    for n, per_seq in zip(group_tiles, seq_tiles):
        x_specs += group_specs(start, n, per_seq)
        start += n
    cast_in, cast_out, cast_shapes = _cast_specs(to_cast, n_tiles, lambda t: t)
    return pl.pallas_call(
        functools.partial(_conv_mixer_kernel, tuple(group_tiles), tuple(seq_tiles), len(to_cast)),
        grid=(n_tiles,),
        in_specs=x_specs + [_resident((1, d)), _resident((d, 3 * d)), _resident((CONV_WIDTH, d)),
                            _resident((1, d)), _resident((d, d))] + cast_in,
        out_specs=[pl.BlockSpec((1, tm, d), lambda t: (t, 0, 0))] + cast_out,
        out_shape=[jax.ShapeDtypeStruct((n_tiles, tm, d), _F32)] + cast_shapes,
        scratch_shapes=[pltpu.VMEM((tm + 2 * HALO_ROWS, d), _F32)],
        compiler_params=_params(("arbitrary",)),
        name="conv_mixer",
    )(*[x for x in groups for _ in range(3)], g, w_in, conv_w, conv_b, w_out, *to_cast)


def _mlp_body(x, g_ref, w_up_ref, w_down_ref):
    xn = _rms_norm(x, g_ref[...]).astype(_BF16)
    acc = x
    for c in range(D_FF // FF_CHUNK):
        cols = slice(c * FF_CHUNK, (c + 1) * FF_CHUNK)
        h = jnp.square(jnp.maximum(_dot(xn, w_up_ref[:, cols]), 0.0)).astype(_BF16)
        acc = acc + _dot(h, w_down_ref[cols, :])
    return acc


def _mlp_kernel(n_cast, x_ref, g_ref, w_up_ref, w_down_ref, *rest):
    o_ref = rest[n_cast]
    _cast_blocks(rest[:n_cast], rest[n_cast + 1:])
    o_ref[...] = _mlp_body(x_ref[...], g_ref, w_up_ref, w_down_ref)


def _proj_mlp_norm_kernel(group_tiles, x_ref, r_ref, sg_ref, w_o_ref, g_ref, w_up_ref, w_down_ref, gf_ref, *o_refs):
    def tile_result():
        proj = None
        for h in range(RET_HEADS):
            cols = slice(h * RET_V_DIM, (h + 1) * RET_V_DIM)
            o = r_ref[:, cols].astype(_F32)
            o = o - jnp.mean(o, axis=-1, keepdims=True)
            o = o * lax.rsqrt(jnp.mean(o * o, axis=-1, keepdims=True) + NORM_EPS)
            gated = (sg_ref[:, cols].astype(_F32) * o).astype(_BF16)
            part = _dot(gated, w_o_ref[cols, :])
            proj = part if proj is None else proj + part
        x = x_ref[...] + proj
        return _rms_norm(_mlp_body(x, g_ref, w_up_ref, w_down_ref), gf_ref[...])

    t = pl.program_id(0)
    start = 0
    for n, o_ref in zip(group_tiles, o_refs):
        @pl.when(jnp.logical_and(t >= start, t < start + n))
        def _(o_ref=o_ref):
            o_ref[...] = tile_result()
        start += n


def _mlp(x, g, w_up, w_down, to_cast=()):
    t, d = x.shape
    tm = MLP_ROW_TILE
    rows = pl.BlockSpec((tm, d), lambda i: (i, 0))
    cast_in, cast_out, cast_shapes = _cast_specs(to_cast, t // tm, lambda i: i)
    return pl.pallas_call(
        functools.partial(_mlp_kernel, len(to_cast)),
        grid=(t // tm,),
        in_specs=[rows, _resident((1, d)), _resident((d, D_FF)), _resident((D_FF, d))] + cast_in,
        out_specs=[rows] + cast_out,
        out_shape=[jax.ShapeDtypeStruct((t, d), _F32)] + cast_shapes,
        compiler_params=_params(("parallel",)),
        name="mlp0",
    )(x, g, w_up, w_down, *to_cast)


def _proj_mlp_norm(stream, r, sg, group_rows, w_o, g, w_up, w_down, g_final):
    t, d = stream.shape
    tm = ROW_TILE
    group_tiles = tuple(n // tm for n in group_rows)
    rows = pl.BlockSpec((tm, d), lambda i: (i, 0))
    wide = pl.BlockSpec((tm, RET_V_WIDTH), lambda i: (i, 0))

    def group_out(start, n):
        return pl.BlockSpec((tm, d), lambda i: (jnp.clip(i - start, 0, n - 1), 0))

    starts = [sum(group_tiles[:gi]) for gi in range(len(group_tiles))]
    return pl.pallas_call(
        functools.partial(_proj_mlp_norm_kernel, group_tiles),
        grid=(t // tm,),
        in_specs=[rows, wide, wide, _resident((RET_V_WIDTH, d)),
                  _resident((1, d)), _resident((d, D_FF)), _resident((D_FF, d)), _resident((1, d))],
        out_specs=[group_out(s0, n) for s0, n in zip(starts, group_tiles)],
        out_shape=[jax.ShapeDtypeStruct((n, d), _F32) for n in group_rows],
        compiler_params=_params(("arbitrary",)),
        name="proj_mlp1_norm",
    )(stream, r, sg, w_o, g, w_up, w_down, g_final)


def _rotary(t, cos, sin):
    half = RET_QK_DIM // 2
    parts = []
    for h in range(RET_HEADS):
        x1 = t[:, h * RET_QK_DIM:h * RET_QK_DIM + half]
        x2 = t[:, h * RET_QK_DIM + half:(h + 1) * RET_QK_DIM]
        parts += [x1 * cos - x2 * sin, x1 * sin + x2 * cos]
    return jnp.concatenate(parts, axis=-1)


def _qkvg_kernel(x_ref, g_ref, w_ref, cos_ref, sin_ref, q_ref, k_ref, v_ref, sg_ref):
    xn = _rms_norm(x_ref[0], g_ref[...]).astype(_BF16)
    cos, sin = cos_ref[...], sin_ref[...]
    qw, vw = RET_QK_WIDTH, RET_V_WIDTH
    gate = _dot(xn, w_ref[:, 2 * qw + vw:2 * qw + 2 * vw])
    half_gate = 0.5 * gate
    sg_ref[0] = (half_gate * (jnp.tanh(half_gate) + 1.0)).astype(_BF16)
    q = _rotary(_dot(xn, w_ref[:, 0:qw]), cos, sin) * (RET_QK_DIM ** -0.5)
    q_ref[0] = q.astype(_BF16)
    k_ref[0] = _rotary(_dot(xn, w_ref[:, qw:2 * qw]), cos, sin).astype(_BF16)
    v_ref[0] = _dot(xn, w_ref[:, 2 * qw:2 * qw + vw]).astype(_BF16)


def _position_tile(t, group_tiles, seq_tiles):
    pos, start = None, 0
    for n, per_seq in zip(group_tiles, seq_tiles):
        here = (t - start) % per_seq
        pos = here if pos is None else jnp.where(t >= start, here, pos)
        start += n
    return pos


def _qkvg(stream, group_tiles, seq_tiles, g, w, cos, sin):
    n_tiles, tm, d = stream.shape
    half = RET_QK_DIM // 2

    def rows(width):
        return pl.BlockSpec((1, tm, width), lambda t: (t, 0, 0))

    def out(width):
        return jax.ShapeDtypeStruct((n_tiles, tm, width), _BF16)

    table = pl.BlockSpec((tm, half), lambda t: (_position_tile(t, group_tiles, seq_tiles), 0))
    return pl.pallas_call(
        _qkvg_kernel,
        grid=(n_tiles,),
        in_specs=[rows(d), _resident((1, d)), _resident((d, 2 * RET_QK_WIDTH + 2 * RET_V_WIDTH)), table, table],
        out_specs=[rows(RET_QK_WIDTH), rows(RET_QK_WIDTH), rows(RET_V_WIDTH), rows(RET_V_WIDTH)],
        out_shape=[out(RET_QK_WIDTH), out(RET_QK_WIDTH), out(RET_V_WIDTH), out(RET_V_WIDTH)],
        compiler_params=_params(("parallel",)),
        name="qkvg_rotary",
    )(stream, g, w, cos, sin)


def _retention_kernel(group_blocks, group_seq, q_ref, k_ref, v_ref, dmat_ref, dqf_ref, dkf_ref, cdf_ref,
                      dqb_ref, dkb_ref, cdb_ref, o_ref):
    c = RET_CHUNK
    block_rows = q_ref.shape[1]
    contract_rows = (((0,), (0,)), ((), ()))
    contract_cols = (((1,), (1,)), ((), ()))

    def decayed_kv(rows, dk_ref):
        kd = (k_ref[0, rows, :].astype(_F32) * dk_ref[0]).astype(_BF16)
        return lax.dot_general(kd, v_ref[0, rows, :], contract_rows, preferred_element_type=_F32)

    def emit(rows, earlier, later):
        qc = q_ref[0, rows, :]
        scores = lax.dot_general(qc, k_ref[0, rows, :], contract_cols, preferred_element_type=_F32) * dmat_ref[0]
        o = _dot(scores.astype(_BF16), v_ref[0, rows, :])
        qf = qc.astype(_F32)
        scaled, states = [], []
        if earlier is not None:
            scaled.append((qf * dqf_ref[0]).astype(_BF16))
            states.append(earlier)
        if later is not None:
            scaled.append((qf * dqb_ref[0]).astype(_BF16))
            states.append(later)
        if scaled:
            o = o + _dot(jnp.concatenate(scaled, axis=1), jnp.concatenate(states, axis=0))
        o_ref[0, rows, :] = o.astype(o_ref.dtype)

    def sequences(seq_len):
        n_chunks = seq_len // c
        for first_row in range(0, block_rows, seq_len):
            def chunk(j):
                return slice(first_row + j * c, first_row + (j + 1) * c)

            earlier = [None] * n_chunks
            later = [None] * n_chunks
            sf = sb = None
            for t in range(n_chunks - 1):
                kv = decayed_kv(chunk(t), dkf_ref)
                sf = kv if sf is None else sf * cdf_ref[0] + kv
                earlier[t + 1] = sf.astype(_BF16)
                j = n_chunks - 1 - t
                kv = decayed_kv(chunk(j), dkb_ref)
                sb = kv if sb is None else sb * cdb_ref[0] + kv
                later[j - 1] = sb.astype(_BF16)
            for j in range(n_chunks):
                emit(chunk(j), earlier[j], later[j])

    block = pl.program_id(0)
    start = 0
    for n, seq_len in zip(group_blocks, group_seq):
        pl.when(jnp.logical_and(block >= start, block < start + n))(functools.partial(sequences, seq_len))
        start += n


def _retention_tables():
    c = RET_CHUNK
    f32 = np.float32
    h = np.arange(RET_HEADS, dtype=f32)
    lg_f = np.log(f32(1.0) - np.power(f32(2.0), f32(-5.0) - h)).astype(f32)[:, None, None]
    lg_b = np.log(f32(1.0) - np.power(f32(2.0), f32(-5.5) - h)).astype(f32)[:, None, None]
    idx = np.arange(c, dtype=f32)
    diff = idx[:, None] - idx[None, :]
    dmat = np.where(diff >= 0, np.exp(lg_f * np.maximum(diff, f32(0.0))), np.exp(lg_b * np.maximum(-diff, f32(0.0))))
    col = idx[None, :, None]

    def wide(t, width):
        return jnp.asarray(np.ascontiguousarray(np.broadcast_to(t.astype(f32), (RET_HEADS, t.shape[1], width))))

    dqf = wide(np.exp(lg_f * (col + f32(1.0))), RET_QK_DIM)
    dkf = wide(np.exp(lg_f * (f32(c - 1.0) - col)), RET_QK_DIM)
    cdf = wide(np.exp(lg_f * f32(c)), RET_V_DIM)
    dqb = wide(np.exp(lg_b * (f32(c) - col)), RET_QK_DIM)
    dkb = wide(np.exp(lg_b * col), RET_QK_DIM)
    cdb = wide(np.exp(lg_b * f32(c)), RET_V_DIM)
    return jnp.asarray(dmat.astype(f32)), dqf, dkf, cdf, dqb, dkb, cdb


def _retention(q, k, v, tables, group_rows, group_seq):
    c = RET_CHUNK
    rows = q.shape[0]
    step = RET_STEP_ROWS
    assert all(n % step == 0 and step % s == 0 for n, s in zip(group_rows, group_seq))

    def blocks(width):
        return pl.BlockSpec((1, step, width), lambda bi, hi: (bi, 0, hi))

    def per_head(n, width):
        return pl.BlockSpec((1, n, width), lambda bi, hi: (hi, 0, 0))

    def blocked(t):
        return t.reshape(-1, step, t.shape[-1])

    r = pl.pallas_call(
        functools.partial(_retention_kernel, tuple(n // step for n in group_rows), tuple(group_seq)),
        grid=(rows // step, RET_HEADS),
        in_specs=[blocks(RET_QK_DIM), blocks(RET_QK_DIM), blocks(RET_V_DIM),
                  per_head(c, c), per_head(c, RET_QK_DIM), per_head(c, RET_QK_DIM), per_head(1, RET_V_DIM),
                  per_head(c, RET_QK_DIM), per_head(c, RET_QK_DIM), per_head(1, RET_V_DIM)],
        out_specs=blocks(RET_V_DIM),
        out_shape=jax.ShapeDtypeStruct((rows // step, step, RET_V_WIDTH), _BF16),
        compiler_params=_params(("parallel", "parallel")),
        name="retention",
    )(blocked(q), blocked(k), blocked(v), *tables)
    return r.reshape(rows, RET_V_WIDTH)


def _rotary_tables(s):
    half = RET_QK_DIM // 2
    f32 = np.float32
    inv = np.power(f32(ROPE_BASE), -np.arange(half, dtype=f32) / f32(half)).astype(f32)
    ang = (np.arange(s, dtype=f32)[:, None] * inv[None, :]).astype(f32)
    return jnp.asarray(np.cos(ang).astype(f32)), jnp.asarray(np.sin(ang).astype(f32))


CAST_IN_CONV = ("w_up_0", "w_down_0", "w_qkvg_1")
CAST_IN_MLP0 = ("w_o_1", "w_up_1", "w_down_1")


def _trunks(groups, p):
    p = dict(p)
    d = groups[0].shape[-1]
    tm = MLP_ROW_TILE
    tiles = [b * s // tm for b, s, _ in (x.shape for x in groups)]
    stream, *cast = _conv_mixer(groups, p["norm_mix_0"], p["w_in_conv_0"], p["conv_w_0"], p["conv_b_0"],
                                p["w_out_conv_0"], [p[n] for n in CAST_IN_CONV])
    p.update(zip(CAST_IN_CONV, cast))
    stream, *cast = _mlp(stream.reshape(-1, d), p["norm_mlp_0"], p["w_up_0"], p["w_down_0"],
                         [p[n] for n in CAST_IN_MLP0])
    p.update(zip(CAST_IN_MLP0, cast))

    cos, sin = _rotary_tables(max(x.shape[1] for x in groups))
    q, k, v, sg = (t.reshape(-1, t.shape[-1]) for t in
                   _qkvg(stream.reshape(-1, tm, d), tiles, [x.shape[1] // tm for x in groups],
                         p["norm_mix_1"], p["w_qkvg_1"], cos, sin))
    r = _retention(q, k, v, _retention_tables(), [n * tm for n in tiles], [x.shape[1] for x in groups])
    ys = _proj_mlp_norm(stream, r, sg, [n * tm for n in tiles], p["w_o_1"], p["norm_mlp_1"], p["w_up_1"],
                        p["w_down_1"], p["norm_final"])
    return tuple(y.reshape(x.shape) for y, x in zip(ys, groups))


def kernel(x_prompt, x_sample, norm_mix_0, w_in_conv_0, conv_w_0, conv_b_0, w_out_conv_0, norm_mlp_0, w_up_0, w_down_0,
           norm_mix_1, w_qkvg_1, w_o_1, norm_mlp_1, w_up_1, w_down_1, norm_final):
    row = lambda t: t.reshape(1, -1)
    p = {
        "norm_mix_0": row(norm_mix_0), "w_in_conv_0": w_in_conv_0.astype(_BF16), "conv_w_0": conv_w_0,
        "conv_b_0": row(conv_b_0), "w_out_conv_0": w_out_conv_0.astype(_BF16),
        "norm_mlp_0": row(norm_mlp_0), "w_up_0": w_up_0, "w_down_0": w_down_0,
        "norm_mix_1": row(norm_mix_1), "w_qkvg_1": w_qkvg_1, "w_o_1": w_o_1,
        "norm_mlp_1": row(norm_mlp_1), "w_up_1": w_up_1, "w_down_1": w_down_1,
        "norm_final": row(norm_final),
    }
    return _trunks((x_prompt, x_sample), p)
```

```python
import functools

import numpy as np
import jax
import jax.numpy as jnp
from jax import lax
from jax.experimental import pallas as pl
from jax.experimental.pallas import tpu as pltpu

D_MODEL = 1024
D_FF = 4 * D_MODEL
CONV_WIDTH = 3
RET_HEADS = 4
RET_QK_DIM = 256
RET_V_DIM = 512
RET_QK_WIDTH = RET_HEADS * RET_QK_DIM
RET_V_WIDTH = RET_HEADS * RET_V_DIM
NORM_EPS = 1e-6
ROPE_BASE = 10000.0

ROW_TILE = 512
MLP_ROW_TILE = 1024
FF_CHUNK = 1024
HALO_ROWS = 8
WEIGHT_STAGE_ROWS = 128
RET_CHUNK = 256
RET_STEP_ROWS = 4096
VMEM_LIMIT_BYTES = 56 * 1024 * 1024

_F32 = jnp.float32
_BF16 = jnp.bfloat16


def _dot(a, b):
    return jnp.dot(a, b, preferred_element_type=_F32)


def _rms_norm(x, g):
    y = x * lax.rsqrt(jnp.mean(x * x, axis=-1, keepdims=True) + NORM_EPS)
    return y * g


def _resident(shape):
    zeros = (0,) * len(shape)
    return pl.BlockSpec(shape, lambda *_: zeros, pipeline_mode=pl.Buffered(1))


def _params(semantics):
    return pltpu.CompilerParams(dimension_semantics=semantics, vmem_limit_bytes=VMEM_LIMIT_BYTES)


def _cast_specs(weights, n_steps, step_of):
    in_specs, out_specs, out_shapes = [], [], []
    for w in weights:
        rows, cols = w.shape
        block = (rows // n_steps, cols)
        in_specs.append(pl.BlockSpec(block, lambda *idx: (step_of(*idx), 0)))
        out_specs.append(pl.BlockSpec(block, lambda *idx: (step_of(*idx), 0)))
        out_shapes.append(jax.ShapeDtypeStruct(w.shape, _BF16))
    return in_specs, out_specs, out_shapes


def _cast_blocks(src_refs, dst_refs):
    for src, dst in zip(src_refs, dst_refs):
        dst[...] = src[...].astype(_BF16)


def _load_as_bf16(src_hbm, dst_ref, stage_ref, sem):
    rows = stage_ref.shape[1]
    n_chunks = src_hbm.shape[0] // rows

    def chunk_copy(c):
        return pltpu.make_async_copy(src_hbm.at[pl.ds(c * rows, rows)], stage_ref.at[c % 2], sem.at[c % 2])

    chunk_copy(0).start()
    for c in range(n_chunks):
        chunk_copy(c).wait()
        if c + 1 < n_chunks:
            chunk_copy(c + 1).start()
        dst_ref[c * rows:(c + 1) * rows, :] = stage_ref[c % 2].astype(_BF16)


def _conv_mixer_kernel(group_tiles, seq_tiles, n_cast, *refs):
    n_groups = len(group_tiles)
    x_refs = [refs[3 * gi:3 * gi + 3] for gi in range(n_groups)]
    g_ref, w_in_hbm, cw_ref, cb_ref, w_out_hbm = refs[3 * n_groups:3 * n_groups + 5]
    rest = refs[3 * n_groups + 5:]
    o_ref = rest[n_cast]
    u_ref, w_in_ref, w_out_ref, in_stage, out_stage, in_sems, out_sems = rest[2 * n_cast + 1:]
    _cast_blocks(rest[:n_cast], rest[n_cast + 1:2 * n_cast + 1])
    t = pl.program_id(0)

    @pl.when(t == 0)
    def _():
        _load_as_bf16(w_in_hbm, w_in_ref, in_stage, in_sems)
        _load_as_bf16(w_out_hbm, w_out_ref, out_stage, out_sems)

    def mix(x_ref, xp_ref, xn_ref, i, n):
        x = x_ref[0]
        tm = x.shape[0]
        ext = tm + 2 * HALO_ROWS
        lo, hi = HALO_ROWS, HALO_ROWS + tm
        y_ext = _rms_norm(jnp.concatenate([xp_ref[0], x, xn_ref[0]], axis=0), g_ref[...])
        xe = y_ext.astype(_BF16)

        d = D_MODEL
        u_ext = _dot(xe, w_in_ref[:, 0:d]) * _dot(xe, w_in_ref[:, 2 * d:3 * d])
        u_ref[0:lo, :] = jnp.where(i > 0, u_ext[0:lo], 0.0)
        u_ref[lo:hi, :] = u_ext[lo:hi]
        u_ref[hi:ext, :] = jnp.where(i < n - 1, u_ext[hi:ext], 0.0)
        u_prev = u_ref[lo - 1:hi - 1, :]
        u_next = u_ref[lo + 1:hi + 1, :]
        z = cb_ref[...] + u_prev * cw_ref[0:1, :] + u_ext[lo:hi] * cw_ref[1:2, :] + u_next * cw_ref[2:3, :]

        gate_b = _dot(xe, w_in_ref[:, d:2 * d])[lo:hi]
        y = _dot((gate_b * z).astype(_BF16), w_out_ref[...])
        o_ref[0] = x + y

    start = 0
    for (x_ref, xp_ref, xn_ref), n, per_seq in zip(x_refs, group_tiles, seq_tiles):
        @pl.when(jnp.logical_and(t >= start, t < start + n))
        def _(x_ref=x_ref, xp_ref=xp_ref, xn_ref=xn_ref, start=start, per_seq=per_seq):
            mix(x_ref, xp_ref, xn_ref, (t - start) % per_seq, per_seq)
        start += n


def _conv_mixer(groups, g, w_in, conv_w, conv_b, w_out, to_cast=()):
    d = groups[0].shape[-1]
    tm = MLP_ROW_TILE
    hb = tm // HALO_ROWS
    group_tiles = [x.shape[0] * x.shape[1] // tm for x in groups]
    seq_tiles = [x.shape[1] // tm for x in groups]
    n_tiles = sum(group_tiles)

    def group_specs(start, n, per_seq):
        last_halo = per_seq * hb - 1

        def where(t):
            local = jnp.clip(t - start, 0, n - 1)
            return local // per_seq, local % per_seq

        def main(t):
            bi, i = where(t)
            return bi, i, 0

        def before(t):
            bi, i = where(t)
            return bi, jnp.maximum(i * hb - 1, 0), 0

        def after(t):
            bi, i = where(t)
            return bi, jnp.minimum((i + 1) * hb, last_halo), 0

        return [pl.BlockSpec((1, tm, d), main), pl.BlockSpec((1, HALO_ROWS, d), before),
                pl.BlockSpec((1, HALO_ROWS, d), after)]

    x_specs, start = [], 0
    for n, per_seq in zip(group_tiles, seq_tiles):
        x_specs += group_specs(start, n, per_seq)
        start += n
    cast_in, cast_out, cast_shapes = _cast_specs(to_cast, n_tiles, lambda t: t)
    return pl.pallas_call(
        functools.partial(_conv_mixer_kernel, tuple(group_tiles), tuple(seq_tiles), len(to_cast)),
        grid=(n_tiles,),
        in_specs=x_specs + [_resident((1, d)), pl.BlockSpec(memory_space=pl.ANY), _resident((CONV_WIDTH, d)),
                            _resident((1, d)), pl.BlockSpec(memory_space=pl.ANY)] + cast_in,
        out_specs=[pl.BlockSpec((1, tm, d), lambda t: (t, 0, 0))] + cast_out,
        out_shape=[jax.ShapeDtypeStruct((n_tiles, tm, d), _F32)] + cast_shapes,
        scratch_shapes=[pltpu.VMEM((tm + 2 * HALO_ROWS, d), _F32),
                        pltpu.VMEM(w_in.shape, _BF16), pltpu.VMEM(w_out.shape, _BF16),
                        pltpu.VMEM((2, WEIGHT_STAGE_ROWS, w_in.shape[1]), _F32),
                        pltpu.VMEM((2, WEIGHT_STAGE_ROWS, w_out.shape[1]), _F32),
                        pltpu.SemaphoreType.DMA((2,)), pltpu.SemaphoreType.DMA((2,))],
        compiler_params=_params(("arbitrary",)),
        name="conv_mixer",
    )(*[x for x in groups for _ in range(3)], g, w_in, conv_w, conv_b, w_out, *to_cast)


def _mlp_body(x, g_ref, w_up_ref, w_down_ref):
    xn = _rms_norm(x, g_ref[...]).astype(_BF16)
    acc = x
    for c in range(D_FF // FF_CHUNK):
        cols = slice(c * FF_CHUNK, (c + 1) * FF_CHUNK)
        h = jnp.square(jnp.maximum(_dot(xn, w_up_ref[:, cols]), 0.0)).astype(_BF16)
        acc = acc + _dot(h, w_down_ref[cols, :])
    return acc


def _mlp_kernel(n_cast, x_ref, g_ref, w_up_ref, w_down_ref, *rest):
    o_ref = rest[n_cast]
    _cast_blocks(rest[:n_cast], rest[n_cast + 1:])
    o_ref[...] = _mlp_body(x_ref[...], g_ref, w_up_ref, w_down_ref)


def _proj_mlp_norm_kernel(group_tiles, x_ref, r_ref, sg_ref, w_o_ref, g_ref, w_up_ref, w_down_ref, gf_ref, *o_refs):
    def tile_result():
        proj = None
        for h in range(RET_HEADS):
            cols = slice(h * RET_V_DIM, (h + 1) * RET_V_DIM)
            o = r_ref[:, cols].astype(_F32)
            o = o - jnp.mean(o, axis=-1, keepdims=True)
            o = o * lax.rsqrt(jnp.mean(o * o, axis=-1, keepdims=True) + NORM_EPS)
            gated = (sg_ref[:, cols].astype(_F32) * o).astype(_BF16)
            part = _dot(gated, w_o_ref[cols, :])
            proj = part if proj is None else proj + part
        x = x_ref[...] + proj
        return _rms_norm(_mlp_body(x, g_ref, w_up_ref, w_down_ref), gf_ref[...])

    t = pl.program_id(0)
    start = 0
    for n, o_ref in zip(group_tiles, o_refs):
        @pl.when(jnp.logical_and(t >= start, t < start + n))
        def _(o_ref=o_ref):
            o_ref[...] = tile_result()
        start += n


def _mlp(x, g, w_up, w_down, to_cast=()):
    t, d = x.shape
    tm = MLP_ROW_TILE
    rows = pl.BlockSpec((tm, d), lambda i: (i, 0))
    cast_in, cast_out, cast_shapes = _cast_specs(to_cast, t // tm, lambda i: i)
    return pl.pallas_call(
        functools.partial(_mlp_kernel, len(to_cast)),
        grid=(t // tm,),
        in_specs=[rows, _resident((1, d)), _resident((d, D_FF)), _resident((D_FF, d))] + cast_in,
        out_specs=[rows] + cast_out,
        out_shape=[jax.ShapeDtypeStruct((t, d), _F32)] + cast_shapes,
        compiler_params=_params(("parallel",)),
        name="mlp0",
    )(x, g, w_up, w_down, *to_cast)


def _proj_mlp_norm(stream, r, sg, group_rows, w_o, g, w_up, w_down, g_final):
    t, d = stream.shape
    tm = ROW_TILE
    group_tiles = tuple(n // tm for n in group_rows)
    rows = pl.BlockSpec((tm, d), lambda i: (i, 0))
    wide = pl.BlockSpec((tm, RET_V_WIDTH), lambda i: (i, 0))

    def group_out(start, n):
        return pl.BlockSpec((tm, d), lambda i: (jnp.clip(i - start, 0, n - 1), 0))

    starts = [sum(group_tiles[:gi]) for gi in range(len(group_tiles))]
    return pl.pallas_call(
        functools.partial(_proj_mlp_norm_kernel, group_tiles),
        grid=(t // tm,),
        in_specs=[rows, wide, wide, _resident((RET_V_WIDTH, d)),
                  _resident((1, d)), _resident((d, D_FF)), _resident((D_FF, d)), _resident((1, d))],
        out_specs=[group_out(s0, n) for s0, n in zip(starts, group_tiles)],
        out_shape=[jax.ShapeDtypeStruct((n, d), _F32) for n in group_rows],
        compiler_params=_params(("arbitrary",)),
        name="proj_mlp1_norm",
    )(stream, r, sg, w_o, g, w_up, w_down, g_final)


def _rotary(t, cos, sin):
    half = RET_QK_DIM // 2
    parts = []
    for h in range(RET_HEADS):
        x1 = t[:, h * RET_QK_DIM:h * RET_QK_DIM + half]
        x2 = t[:, h * RET_QK_DIM + half:(h + 1) * RET_QK_DIM]
        parts += [x1 * cos - x2 * sin, x1 * sin + x2 * cos]
    return jnp.concatenate(parts, axis=-1)


def _qkvg_kernel(x_ref, g_ref, w_ref, cos_ref, sin_ref, q_ref, k_ref, v_ref, sg_ref):
    xn = _rms_norm(x_ref[0], g_ref[...]).astype(_BF16)
    cos, sin = cos_ref[...], sin_ref[...]
    qw, vw = RET_QK_WIDTH, RET_V_WIDTH
    gate = _dot(xn, w_ref[:, 2 * qw + vw:2 * qw + 2 * vw])
    half_gate = 0.5 * gate
    sg_ref[0] = (half_gate * (jnp.tanh(half_gate) + 1.0)).astype(_BF16)
    q = _rotary(_dot(xn, w_ref[:, 0:qw]), cos, sin) * (RET_QK_DIM ** -0.5)
    q_ref[0] = q.astype(_BF16)
    k_ref[0] = _rotary(_dot(xn, w_ref[:, qw:2 * qw]), cos, sin).astype(_BF16)
    v_ref[0] = _dot(xn, w_ref[:, 2 * qw:2 * qw + vw]).astype(_BF16)


def _position_tile(t, group_tiles, seq_tiles):
    pos, start = None, 0
    for n, per_seq in zip(group_tiles, seq_tiles):
        here = (t - start) % per_seq
        pos = here if pos is None else jnp.where(t >= start, here, pos)
        start += n
    return pos


def _qkvg(stream, group_tiles, seq_tiles, g, w, cos, sin):
    n_tiles, tm, d = stream.shape
    half = RET_QK_DIM // 2

    def rows(width):
        return pl.BlockSpec((1, tm, width), lambda t: (t, 0, 0))

    def out(width):
        return jax.ShapeDtypeStruct((n_tiles, tm, width), _BF16)

    table = pl.BlockSpec((tm, half), lambda t: (_position_tile(t, group_tiles, seq_tiles), 0))
    return pl.pallas_call(
        _qkvg_kernel,
        grid=(n_tiles,),
        in_specs=[rows(d), _resident((1, d)), _resident((d, 2 * RET_QK_WIDTH + 2 * RET_V_WIDTH)), table, table],
        out_specs=[rows(RET_QK_WIDTH), rows(RET_QK_WIDTH), rows(RET_V_WIDTH), rows(RET_V_WIDTH)],
        out_shape=[out(RET_QK_WIDTH), out(RET_QK_WIDTH), out(RET_V_WIDTH), out(RET_V_WIDTH)],
        compiler_params=_params(("parallel",)),
        name="qkvg_rotary",
    )(stream, g, w, cos, sin)


def _retention_kernel(group_blocks, group_seq, q_ref, k_ref, v_ref, dmat_ref, dqf_ref, dkf_ref, cdf_ref,
                      dqb_ref, dkb_ref, cdb_ref, o_ref):
    c = RET_CHUNK
    block_rows = q_ref.shape[1]
    contract_rows = (((0,), (0,)), ((), ()))
    contract_cols = (((1,), (1,)), ((), ()))

    def decayed_kv(rows, dk_ref):
        kd = (k_ref[0, rows, :].astype(_F32) * dk_ref[0]).astype(_BF16)
        return lax.dot_general(kd, v_ref[0, rows, :], contract_rows, preferred_element_type=_F32)

    def emit(rows, earlier, later):
        qc = q_ref[0, rows, :]
        scores = lax.dot_general(qc, k_ref[0, rows, :], contract_cols, preferred_element_type=_F32) * dmat_ref[0]
        o = _dot(scores.astype(_BF16), v_ref[0, rows, :])
        qf = qc.astype(_F32)
        scaled, states = [], []
        if earlier is not None:
            scaled.append((qf * dqf_ref[0]).astype(_BF16))
            states.append(earlier)
        if later is not None:
            scaled.append((qf * dqb_ref[0]).astype(_BF16))
            states.append(later)
        if scaled:
            o = o + _dot(jnp.concatenate(scaled, axis=1), jnp.concatenate(states, axis=0))
        o_ref[0, rows, :] = o.astype(o_ref.dtype)

    def sequences(seq_len):
        n_chunks = seq_len // c
        for first_row in range(0, block_rows, seq_len):
            def chunk(j):
                return slice(first_row + j * c, first_row + (j + 1) * c)

            earlier = [None] * n_chunks
            later = [None] * n_chunks
            sf = sb = None
            for t in range(n_chunks - 1):
                kv = decayed_kv(chunk(t), dkf_ref)
                sf = kv if sf is None else sf * cdf_ref[0] + kv
                earlier[t + 1] = sf.astype(_BF16)
                j = n_chunks - 1 - t
                kv = decayed_kv(chunk(j), dkb_ref)
                sb = kv if sb is None else sb * cdb_ref[0] + kv
                later[j - 1] = sb.astype(_BF16)
            for j in range(n_chunks):
                emit(chunk(j), earlier[j], later[j])

    block = pl.program_id(0)
    start = 0
    for n, seq_len in zip(group_blocks, group_seq):
        pl.when(jnp.logical_and(block >= start, block < start + n))(functools.partial(sequences, seq_len))
        start += n


def _retention_tables():
    c = RET_CHUNK
    f32 = np.float32
    h = np.arange(RET_HEADS, dtype=f32)
    lg_f = np.log(f32(1.0) - np.power(f32(2.0), f32(-5.0) - h)).astype(f32)[:, None, None]
    lg_b = np.log(f32(1.0) - np.power(f32(2.0), f32(-5.5) - h)).astype(f32)[:, None, None]
    idx = np.arange(c, dtype=f32)
    diff = idx[:, None] - idx[None, :]
    dmat = np.where(diff >= 0, np.exp(lg_f * np.maximum(diff, f32(0.0))), np.exp(lg_b * np.maximum(-diff, f32(0.0))))
    col = idx[None, :, None]

    def wide(t, width):
        return jnp.asarray(np.ascontiguousarray(np.broadcast_to(t.astype(f32), (RET_HEADS, t.shape[1], width))))

    dqf = wide(np.exp(lg_f * (col + f32(1.0))), RET_QK_DIM)
    dkf = wide(np.exp(lg_f * (f32(c - 1.0) - col)), RET_QK_DIM)
    cdf = wide(np.exp(lg_f * f32(c)), RET_V_DIM)
    dqb = wide(np.exp(lg_b * (f32(c) - col)), RET_QK_DIM)
    dkb = wide(np.exp(lg_b * col), RET_QK_DIM)
    cdb = wide(np.exp(lg_b * f32(c)), RET_V_DIM)
    return jnp.asarray(dmat.astype(f32)), dqf, dkf, cdf, dqb, dkb, cdb


def _retention(q, k, v, tables, group_rows, group_seq):
    c = RET_CHUNK
    rows = q.shape[0]
    step = RET_STEP_ROWS
    assert all(n % step == 0 and step % s == 0 for n, s in zip(group_rows, group_seq))

    def blocks(width):
        return pl.BlockSpec((1, step, width), lambda bi, hi: (bi, 0, hi))

    def per_head(n, width):
        return pl.BlockSpec((1, n, width), lambda bi, hi: (hi, 0, 0))

    def blocked(t):
        return t.reshape(-1, step, t.shape[-1])

    r = pl.pallas_call(
        functools.partial(_retention_kernel, tuple(n // step for n in group_rows), tuple(group_seq)),
        grid=(rows // step, RET_HEADS),
        in_specs=[blocks(RET_QK_DIM), blocks(RET_QK_DIM), blocks(RET_V_DIM),
                  per_head(c, c), per_head(c, RET_QK_DIM), per_head(c, RET_QK_DIM), per_head(1, RET_V_DIM),
                  per_head(c, RET_QK_DIM), per_head(c, RET_QK_DIM), per_head(1, RET_V_DIM)],
        out_specs=blocks(RET_V_DIM),
        out_shape=jax.ShapeDtypeStruct((rows // step, step, RET_V_WIDTH), _BF16),
        compiler_params=_params(("parallel", "parallel")),
        name="retention",
    )(blocked(q), blocked(k), blocked(v), *tables)
    return r.reshape(rows, RET_V_WIDTH)


def _rotary_tables(s):
    half = RET_QK_DIM // 2
    f32 = np.float32
    inv = np.power(f32(ROPE_BASE), -np.arange(half, dtype=f32) / f32(half)).astype(f32)
    ang = (np.arange(s, dtype=f32)[:, None] * inv[None, :]).astype(f32)
    return jnp.asarray(np.cos(ang).astype(f32)), jnp.asarray(np.sin(ang).astype(f32))


CAST_IN_CONV = ("w_up_0", "w_down_0", "w_qkvg_1")
CAST_IN_MLP0 = ("w_o_1", "w_up_1", "w_down_1")


def _trunks(groups, p):
    p = dict(p)
    d = groups[0].shape[-1]
    tm = MLP_ROW_TILE
    tiles = [b * s // tm for b, s, _ in (x.shape for x in groups)]
    stream, *cast = _conv_mixer(groups, p["norm_mix_0"], p["w_in_conv_0"], p["conv_w_0"], p["conv_b_0"],
                                p["w_out_conv_0"], [p[n] for n in CAST_IN_CONV])
    p.update(zip(CAST_IN_CONV, cast))
    stream, *cast = _mlp(stream.reshape(-1, d), p["norm_mlp_0"], p["w_up_0"], p["w_down_0"],
                         [p[n] for n in CAST_IN_MLP0])
    p.update(zip(CAST_IN_MLP0, cast))

    cos, sin = _rotary_tables(max(x.shape[1] for x in groups))
    q, k, v, sg = (t.reshape(-1, t.shape[-1]) for t in
                   _qkvg(stream.reshape(-1, tm, d), tiles, [x.shape[1] // tm for x in groups],
                         p["norm_mix_1"], p["w_qkvg_1"], cos, sin))
    r = _retention(q, k, v, _retention_tables(), [n * tm for n in tiles], [x.shape[1] for x in groups])
    ys = _proj_mlp_norm(stream, r, sg, [n * tm for n in tiles], p["w_o_1"], p["norm_mlp_1"], p["w_up_1"],
                        p["w_down_1"], p["norm_final"])
    return tuple(y.reshape(x.shape) for y, x in zip(ys, groups))


def kernel(x_prompt, x_sample, norm_mix_0, w_in_conv_0, conv_w_0, conv_b_0, w_out_conv_0, norm_mlp_0, w_up_0, w_down_0,
           norm_mix_1, w_qkvg_1, w_o_1, norm_mlp_1, w_up_1, w_down_1, norm_final):
    row = lambda t: t.reshape(1, -1)
    p = {
        "norm_mix_0": row(norm_mix_0), "w_in_conv_0": w_in_conv_0, "conv_w_0": conv_w_0,
        "conv_b_0": row(conv_b_0), "w_out_conv_0": w_out_conv_0,
        "norm_mlp_0": row(norm_mlp_0), "w_up_0": w_up_0, "w_down_0": w_down_0,
        "norm_mix_1": row(norm_mix_1), "w_qkvg_1": w_qkvg_1, "w_o_1": w_o_1,
        "norm_mlp_1": row(norm_mlp_1), "w_up_1": w_up_1, "w_down_1": w_down_1,
        "norm_final": row(norm_final),
    }
    return _trunks((x_prompt, x_sample), p)
```

```python
import functools

import numpy as np
import jax
import jax.numpy as jnp
from jax import lax
from jax.experimental import pallas as pl
from jax.experimental.pallas import tpu as pltpu

D_MODEL = 1024
D_FF = 4 * D_MODEL
CONV_WIDTH = 3
RET_HEADS = 4
RET_QK_DIM = 256
RET_V_DIM = 512
RET_QK_WIDTH = RET_HEADS * RET_QK_DIM
RET_V_WIDTH = RET_HEADS * RET_V_DIM
NORM_EPS = 1e-6
ROPE_BASE = 10000.0

ROW_TILE = 512
MLP_ROW_TILE = 1024
FF_CHUNK = 1024
HALO_ROWS = 8
RET_CHUNK = 256
RET_STEP_ROWS = 4096
VMEM_LIMIT_BYTES = 56 * 1024 * 1024

_F32 = jnp.float32
_BF16 = jnp.bfloat16


def _dot(a, b):
    return jnp.dot(a, b, preferred_element_type=_F32)


def _rms_norm(x, g):
    y = x * lax.rsqrt(jnp.mean(x * x, axis=-1, keepdims=True) + NORM_EPS)
    return y * g


def _resident(shape):
    zeros = (0,) * len(shape)
    return pl.BlockSpec(shape, lambda *_: zeros, pipeline_mode=pl.Buffered(1))


def _params(semantics):
    return pltpu.CompilerParams(dimension_semantics=semantics, vmem_limit_bytes=VMEM_LIMIT_BYTES)


def _cast_specs(weights, n_steps):
    in_specs, out_specs, out_shapes = [], [], []
    for w in weights:
        rows, cols = w.shape
        block = (rows // n_steps, cols)
        in_specs.append(pl.BlockSpec(block, lambda t: (t, 0)))
        out_specs.append(pl.BlockSpec(block, lambda t: (t, 0)))
        out_shapes.append(jax.ShapeDtypeStruct(w.shape, _BF16))
    return in_specs, out_specs, out_shapes


def _cast_blocks(src_refs, dst_refs):
    for src, dst in zip(src_refs, dst_refs):
        dst[...] = src[...].astype(_BF16)


def _conv_mixer_kernel(group_tiles, seq_tiles, n_cast, *refs):
    n_groups = len(group_tiles)
    x_refs = [refs[3 * gi:3 * gi + 3] for gi in range(n_groups)]
    g_ref, w_in_ref, cw_ref, cb_ref, w_out_ref = refs[3 * n_groups:3 * n_groups + 5]
    rest = refs[3 * n_groups + 5:]
    o_ref, u_ref = rest[n_cast], rest[-1]
    _cast_blocks(rest[:n_cast], rest[n_cast + 1:-1])

    def mix(x_ref, xp_ref, xn_ref, i, n):
        x = x_ref[0]
        tm = x.shape[0]
        ext = tm + 2 * HALO_ROWS
        lo, hi = HALO_ROWS, HALO_ROWS + tm
        y_ext = _rms_norm(jnp.concatenate([xp_ref[0], x, xn_ref[0]], axis=0), g_ref[...])
        xe = y_ext.astype(_BF16)

        d = D_MODEL
        u_ext = _dot(xe, w_in_ref[:, 0:d]) * _dot(xe, w_in_ref[:, 2 * d:3 * d])
        u_ref[0:lo, :] = jnp.where(i > 0, u_ext[0:lo], 0.0)
        u_ref[lo:hi, :] = u_ext[lo:hi]
        u_ref[hi:ext, :] = jnp.where(i < n - 1, u_ext[hi:ext], 0.0)
        u_prev = u_ref[lo - 1:hi - 1, :]
        u_next = u_ref[lo + 1:hi + 1, :]
        z = cb_ref[...] + u_prev * cw_ref[0:1, :] + u_ext[lo:hi] * cw_ref[1:2, :] + u_next * cw_ref[2:3, :]

        gate_b = _dot(xe, w_in_ref[:, d:2 * d])[lo:hi]
        y = _dot((gate_b * z).astype(_BF16), w_out_ref[...])
        o_ref[0] = x + y

    t = pl.program_id(0)
    start = 0
    for (x_ref, xp_ref, xn_ref), n, per_seq in zip(x_refs, group_tiles, seq_tiles):
        @pl.when(jnp.logical_and(t >= start, t < start + n))
        def _(x_ref=x_ref, xp_ref=xp_ref, xn_ref=xn_ref, start=start, per_seq=per_seq):
            mix(x_ref, xp_ref, xn_ref, (t - start) % per_seq, per_seq)
        start += n


def _conv_mixer(groups, g, w_in, conv_w, conv_b, w_out, to_cast=()):
    d = groups[0].shape[-1]
    tm = MLP_ROW_TILE
    hb = tm // HALO_ROWS
    group_tiles = [x.shape[0] * x.shape[1] // tm for x in groups]
    seq_tiles = [x.shape[1] // tm for x in groups]
    n_tiles = sum(group_tiles)

    def group_specs(start, n, per_seq):
        last_halo = per_seq * hb - 1

        def where(t):
            local = jnp.clip(t - start, 0, n - 1)
            return local // per_seq, local % per_seq

        def main(t):
            bi, i = where(t)
            return bi, i, 0

        def before(t):
            bi, i = where(t)
            return bi, jnp.maximum(i * hb - 1, 0), 0

        def after(t):
            bi, i = where(t)
            return bi, jnp.minimum((i + 1) * hb, last_halo), 0

        return [pl.BlockSpec((1, tm, d), main), pl.BlockSpec((1, HALO_ROWS, d), before),
                pl.BlockSpec((1, HALO_ROWS, d), after)]

    x_specs, start = [], 0
    for n, per_seq in zip(group_tiles, seq_tiles):
        x_specs += group_specs(start, n, per_seq)
        start += n
    cast_in, cast_out, cast_shapes = _cast_specs(to_cast, n_tiles)
    return pl.pallas_call(
        functools.partial(_conv_mixer_kernel, tuple(group_tiles), tuple(seq_tiles), len(to_cast)),
        grid=(n_tiles,),
        in_specs=x_specs + [_resident((1, d)), _resident((d, 3 * d)), _resident((CONV_WIDTH, d)),
                            _resident((1, d)), _resident((d, d))] + cast_in,
        out_specs=[pl.BlockSpec((1, tm, d), lambda t: (t, 0, 0))] + cast_out,
        out_shape=[jax.ShapeDtypeStruct((n_tiles, tm, d), _F32)] + cast_shapes,
        scratch_shapes=[pltpu.VMEM((tm + 2 * HALO_ROWS, d), _F32)],
        compiler_params=_params(("arbitrary",)),
        name="conv_mixer",
    )(*[x for x in groups for _ in range(3)], g, w_in, conv_w, conv_b, w_out, *to_cast)


def _mlp_body(x, g_ref, w_up_ref, w_down_ref):
    xn = _rms_norm(x, g_ref[...]).astype(_BF16)
    acc = x
    for c in range(D_FF // FF_CHUNK):
        cols = slice(c * FF_CHUNK, (c + 1) * FF_CHUNK)
        h = jnp.square(jnp.maximum(_dot(xn, w_up_ref[:, cols]), 0.0)).astype(_BF16)
        acc = acc + _dot(h, w_down_ref[cols, :])
    return acc


def _mlp_kernel(n_cast, x_ref, g_ref, w_up_ref, w_down_ref, *rest):
    o_ref = rest[n_cast]
    _cast_blocks(rest[:n_cast], rest[n_cast + 1:])
    o_ref[...] = _mlp_body(x_ref[...], g_ref, w_up_ref, w_down_ref)


def _proj_mlp_norm_kernel(group_tiles, x_ref, r_ref, sg_ref, w_o_ref, g_ref, w_up_ref, w_down_ref, gf_ref, *o_refs):
    def tile_result():
        proj = None
        for h in range(RET_HEADS):
            cols = slice(h * RET_V_DIM, (h + 1) * RET_V_DIM)
            o = r_ref[:, cols].astype(_F32)
            o = o - jnp.mean(o, axis=-1, keepdims=True)
            o = o * lax.rsqrt(jnp.mean(o * o, axis=-1, keepdims=True) + NORM_EPS)
            gated = (sg_ref[:, cols].astype(_F32) * o).astype(_BF16)
            part = _dot(gated, w_o_ref[cols, :])
            proj = part if proj is None else proj + part
        x = x_ref[...] + proj
        return _rms_norm(_mlp_body(x, g_ref, w_up_ref, w_down_ref), gf_ref[...])

    t = pl.program_id(0)
    start = 0
    for n, o_ref in zip(group_tiles, o_refs):
        @pl.when(jnp.logical_and(t >= start, t < start + n))
        def _(o_ref=o_ref):
            o_ref[...] = tile_result()
        start += n


def _mlp(x, g, w_up, w_down, to_cast=()):
    t, d = x.shape
    tm = MLP_ROW_TILE
    rows = pl.BlockSpec((tm, d), lambda i: (i, 0))
    cast_in, cast_out, cast_shapes = _cast_specs(to_cast, t // tm)
    return pl.pallas_call(
        functools.partial(_mlp_kernel, len(to_cast)),
        grid=(t // tm,),
        in_specs=[rows, _resident((1, d)), _resident((d, D_FF)), _resident((D_FF, d))] + cast_in,
        out_specs=[rows] + cast_out,
        out_shape=[jax.ShapeDtypeStruct((t, d), _F32)] + cast_shapes,
        compiler_params=_params(("parallel",)),
        name="mlp0",
    )(x, g, w_up, w_down, *to_cast)


def _proj_mlp_norm(stream, r, sg, group_rows, w_o, g, w_up, w_down, g_final):
    t, d = stream.shape
    tm = ROW_TILE
    group_tiles = tuple(n // tm for n in group_rows)
    rows = pl.BlockSpec((tm, d), lambda i: (i, 0))
    wide = pl.BlockSpec((tm, RET_V_WIDTH), lambda i: (i, 0))

    def group_out(start, n):
        return pl.BlockSpec((tm, d), lambda i: (jnp.clip(i - start, 0, n - 1), 0))

    starts = [sum(group_tiles[:gi]) for gi in range(len(group_tiles))]
    return pl.pallas_call(
        functools.partial(_proj_mlp_norm_kernel, group_tiles),
        grid=(t // tm,),
        in_specs=[rows, wide, wide, _resident((RET_V_WIDTH, d)),
                  _resident((1, d)), _resident((d, D_FF)), _resident((D_FF, d)), _resident((1, d))],
        out_specs=[group_out(s0, n) for s0, n in zip(starts, group_tiles)],
        out_shape=[jax.ShapeDtypeStruct((n, d), _F32) for n in group_rows],
        compiler_params=_params(("arbitrary",)),
        name="proj_mlp1_norm",
    )(stream, r, sg, w_o, g, w_up, w_down, g_final)


def _rotary(t, cos, sin):
    half = RET_QK_DIM // 2
    parts = []
    for h in range(RET_HEADS):
        x1 = t[:, h * RET_QK_DIM:h * RET_QK_DIM + half]
        x2 = t[:, h * RET_QK_DIM + half:(h + 1) * RET_QK_DIM]
        parts += [x1 * cos - x2 * sin, x1 * sin + x2 * cos]
    return jnp.concatenate(parts, axis=-1)


def _qkvg_kernel(x_ref, g_ref, w_ref, cos_ref, sin_ref, q_ref, k_ref, v_ref, sg_ref):
    xn = _rms_norm(x_ref[0], g_ref[...]).astype(_BF16)
    cos, sin = cos_ref[...], sin_ref[...]
    qw, vw = RET_QK_WIDTH, RET_V_WIDTH
    gate = _dot(xn, w_ref[:, 2 * qw + vw:2 * qw + 2 * vw])
    half_gate = 0.5 * gate
    sg_ref[0] = (half_gate * (jnp.tanh(half_gate) + 1.0)).astype(_BF16)
    q = _rotary(_dot(xn, w_ref[:, 0:qw]), cos, sin) * (RET_QK_DIM ** -0.5)
    q_ref[0] = q.astype(_BF16)
    k_ref[0] = _rotary(_dot(xn, w_ref[:, qw:2 * qw]), cos, sin).astype(_BF16)
    v_ref[0] = _dot(xn, w_ref[:, 2 * qw:2 * qw + vw]).astype(_BF16)


def _position_tile(t, group_tiles, seq_tiles):
    pos, start = None, 0
    for n, per_seq in zip(group_tiles, seq_tiles):
        here = (t - start) % per_seq
        pos = here if pos is None else jnp.where(t >= start, here, pos)
        start += n
    return pos


def _qkvg(stream, group_tiles, seq_tiles, g, w, cos, sin):
    n_tiles, tm, d = stream.shape
    half = RET_QK_DIM // 2

    def rows(width):
        return pl.BlockSpec((1, tm, width), lambda t: (t, 0, 0))

    def out(width):
        return jax.ShapeDtypeStruct((n_tiles, tm, width), _BF16)

    table = pl.BlockSpec((tm, half), lambda t: (_position_tile(t, group_tiles, seq_tiles), 0))
    return pl.pallas_call(
        _qkvg_kernel,
        grid=(n_tiles,),
        in_specs=[rows(d), _resident((1, d)), _resident((d, 2 * RET_QK_WIDTH + 2 * RET_V_WIDTH)), table, table],
        out_specs=[rows(RET_QK_WIDTH), rows(RET_QK_WIDTH), rows(RET_V_WIDTH), rows(RET_V_WIDTH)],
        out_shape=[out(RET_QK_WIDTH), out(RET_QK_WIDTH), out(RET_V_WIDTH), out(RET_V_WIDTH)],
        compiler_params=_params(("parallel",)),
        name="qkvg_rotary",
    )(stream, g, w, cos, sin)


def _retention_kernel(group_blocks, group_seq, q_ref, k_ref, v_ref, dmat_ref, dqf_ref, dkf_ref, cdf_ref,
                      dqb_ref, dkb_ref, cdb_ref, o_ref):
    c = RET_CHUNK
    block_rows = q_ref.shape[1]
    contract_rows = (((0,), (0,)), ((), ()))
    contract_cols = (((1,), (1,)), ((), ()))

    def decayed_kv(rows, dk_ref):
        kd = (k_ref[0, rows, :].astype(_F32) * dk_ref[0]).astype(_BF16)
        return lax.dot_general(kd, v_ref[0, rows, :], contract_rows, preferred_element_type=_F32)

    def emit(rows, earlier, later):
        qc = q_ref[0, rows, :]
        scores = lax.dot_general(qc, k_ref[0, rows, :], contract_cols, preferred_element_type=_F32) * dmat_ref[0]
        o = _dot(scores.astype(_BF16), v_ref[0, rows, :])
        qf = qc.astype(_F32)
        scaled, states = [], []
        if earlier is not None:
            scaled.append((qf * dqf_ref[0]).astype(_BF16))
            states.append(earlier)
        if later is not None:
            scaled.append((qf * dqb_ref[0]).astype(_BF16))
            states.append(later)
        if scaled:
            o = o + _dot(jnp.concatenate(scaled, axis=1), jnp.concatenate(states, axis=0))
        o_ref[0, rows, :] = o.astype(o_ref.dtype)

    def sequences(seq_len):
        n_chunks = seq_len // c
        for first_row in range(0, block_rows, seq_len):
            def chunk(j):
                return slice(first_row + j * c, first_row + (j + 1) * c)

            earlier = [None] * n_chunks
            later = [None] * n_chunks
            sf = sb = None
            for t in range(n_chunks - 1):
                kv = decayed_kv(chunk(t), dkf_ref)
                sf = kv if sf is None else sf * cdf_ref[0] + kv
                earlier[t + 1] = sf.astype(_BF16)
                j = n_chunks - 1 - t
                kv = decayed_kv(chunk(j), dkb_ref)
                sb = kv if sb is None else sb * cdb_ref[0] + kv
                later[j - 1] = sb.astype(_BF16)
            for j in range(n_chunks):
                emit(chunk(j), earlier[j], later[j])

    block = pl.program_id(0)
    start = 0
    for n, seq_len in zip(group_blocks, group_seq):
        pl.when(jnp.logical_and(block >= start, block < start + n))(functools.partial(sequences, seq_len))
        start += n


def _retention_tables():
    c = RET_CHUNK
    f32 = np.float32
    h = np.arange(RET_HEADS, dtype=f32)
    lg_f = np.log(f32(1.0) - np.power(f32(2.0), f32(-5.0) - h)).astype(f32)[:, None, None]
    lg_b = np.log(f32(1.0) - np.power(f32(2.0), f32(-5.5) - h)).astype(f32)[:, None, None]
    idx = np.arange(c, dtype=f32)
    diff = idx[:, None] - idx[None, :]
    dmat = np.where(diff >= 0, np.exp(lg_f * np.maximum(diff, f32(0.0))), np.exp(lg_b * np.maximum(-diff, f32(0.0))))
    col = idx[None, :, None]

    def wide(t, width):
        return jnp.asarray(np.ascontiguousarray(np.broadcast_to(t.astype(f32), (RET_HEADS, t.shape[1], width))))

    dqf = wide(np.exp(lg_f * (col + f32(1.0))), RET_QK_DIM)
    dkf = wide(np.exp(lg_f * (f32(c - 1.0) - col)), RET_QK_DIM)
    cdf = wide(np.exp(lg_f * f32(c)), RET_V_DIM)
    dqb = wide(np.exp(lg_b * (f32(c) - col)), RET_QK_DIM)
    dkb = wide(np.exp(lg_b * col), RET_QK_DIM)
    cdb = wide(np.exp(lg_b * f32(c)), RET_V_DIM)
    return jnp.asarray(dmat.astype(f32)), dqf, dkf, cdf, dqb, dkb, cdb


def _retention(q, k, v, tables, group_rows, group_seq):
    c = RET_CHUNK
    rows = q.shape[0]
    step = RET_STEP_ROWS
    assert all(n % step == 0 and step % s == 0 for n, s in zip(group_rows, group_seq))

    def blocks(width):
        return pl.BlockSpec((1, step, width), lambda bi, hi: (bi, 0, hi))

    def per_head(n, width):
        return pl.BlockSpec((1, n, width), lambda bi, hi: (hi, 0, 0))

    def blocked(t):
        return t.reshape(-1, step, t.shape[-1])

    r = pl.pallas_call(
        functools.partial(_retention_kernel, tuple(n // step for n in group_rows), tuple(group_seq)),
        grid=(rows // step, RET_HEADS),
        in_specs=[blocks(RET_QK_DIM), blocks(RET_QK_DIM), blocks(RET_V_DIM),
                  per_head(c, c), per_head(c, RET_QK_DIM), per_head(c, RET_QK_DIM), per_head(1, RET_V_DIM),
                  per_head(c, RET_QK_DIM), per_head(c, RET_QK_DIM), per_head(1, RET_V_DIM)],
        out_specs=blocks(RET_V_DIM),
        out_shape=jax.ShapeDtypeStruct((rows // step, step, RET_V_WIDTH), _BF16),
        compiler_params=_params(("parallel", "parallel")),
        name="retention",
    )(blocked(q), blocked(k), blocked(v), *tables)
    return r.reshape(rows, RET_V_WIDTH)


def _rotary_tables(s):
    half = RET_QK_DIM // 2
    f32 = np.float32
    inv = np.power(f32(ROPE_BASE), -np.arange(half, dtype=f32) / f32(half)).astype(f32)
    ang = (np.arange(s, dtype=f32)[:, None] * inv[None, :]).astype(f32)
    return jnp.asarray(np.cos(ang).astype(f32)), jnp.asarray(np.sin(ang).astype(f32))


CAST_IN_CONV = ("w_up_0", "w_down_0", "w_qkvg_1")
CAST_IN_MLP0 = ("w_o_1", "w_up_1", "w_down_1")


def _trunks(groups, p):
    p = dict(p)
    d = groups[0].shape[-1]
    tm = MLP_ROW_TILE
    tiles = [b * s // tm for b, s, _ in (x.shape for x in groups)]
    stream, *cast = _conv_mixer(groups, p["norm_mix_0"], p["w_in_conv_0"], p["conv_w_0"], p["conv_b_0"],
                                p["w_out_conv_0"], [p[n] for n in CAST_IN_CONV])
    p.update(zip(CAST_IN_CONV, cast))
    stream, *cast = _mlp(stream.reshape(-1, d), p["norm_mlp_0"], p["w_up_0"], p["w_down_0"],
                         [p[n] for n in CAST_IN_MLP0])
    p.update(zip(CAST_IN_MLP0, cast))

    cos, sin = _rotary_tables(max(x.shape[1] for x in groups))
    q, k, v, sg = (t.reshape(-1, t.shape[-1]) for t in
                   _qkvg(stream.reshape(-1, tm, d), tiles, [x.shape[1] // tm for x in groups],
                         p["norm_mix_1"], p["w_qkvg_1"], cos, sin))
    r = _retention(q, k, v, _retention_tables(), [n * tm for n in tiles], [x.shape[1] for x in groups])
    ys = _proj_mlp_norm(stream, r, sg, [n * tm for n in tiles], p["w_o_1"], p["norm_mlp_1"], p["w_up_1"],
                        p["w_down_1"], p["norm_final"])
    return tuple(y.reshape(x.shape) for y, x in zip(ys, groups))


def kernel(x_prompt, x_sample, norm_mix_0, w_in_conv_0, conv_w_0, conv_b_0, w_out_conv_0, norm_mlp_0, w_up_0, w_down_0,
           norm_mix_1, w_qkvg_1, w_o_1, norm_mlp_1, w_up_1, w_down_1, norm_final):
    row = lambda t: t.reshape(1, -1)
    p = {
        "norm_mix_0": row(norm_mix_0), "w_in_conv_0": w_in_conv_0.astype(_BF16), "conv_w_0": conv_w_0,
        "conv_b_0": row(conv_b_0), "w_out_conv_0": w_out_conv_0.astype(_BF16),
        "norm_mlp_0": row(norm_mlp_0), "w_up_0": w_up_0, "w_down_0": w_down_0,
        "norm_mix_1": row(norm_mix_1), "w_qkvg_1": w_qkvg_1, "w_o_1": w_o_1,
        "norm_mlp_1": row(norm_mlp_1), "w_up_1": w_up_1, "w_down_1": w_down_1,
        "norm_final": row(norm_final),
    }
    return _trunks((x_prompt, x_sample), p)
```

```python
import functools

import numpy as np
import jax
import jax.numpy as jnp
from jax import lax
from jax.experimental import pallas as pl
from jax.experimental.pallas import tpu as pltpu

D_MODEL = 1024
D_FF = 4 * D_MODEL
CONV_WIDTH = 3
RET_HEADS = 4
RET_QK_DIM = 256
RET_V_DIM = 512
RET_QK_WIDTH = RET_HEADS * RET_QK_DIM
RET_V_WIDTH = RET_HEADS * RET_V_DIM
NORM_EPS = 1e-6
ROPE_BASE = 10000.0

ROW_TILE = 512
MLP_ROW_TILE = 1024
FF_CHUNK = 1024
HALO_ROWS = 8
RET_CHUNK = 256
RET_STEP_ROWS = 4096
VMEM_LIMIT_BYTES = 56 * 1024 * 1024

_F32 = jnp.float32
_BF16 = jnp.bfloat16


def _dot(a, b):
    return jnp.dot(a, b, preferred_element_type=_F32)


def _rms_norm(x, g):
    y = x * lax.rsqrt(jnp.mean(x * x, axis=-1, keepdims=True) + NORM_EPS)
    return y * g


def _resident(shape):
    zeros = (0,) * len(shape)
    return pl.BlockSpec(shape, lambda *_: zeros, pipeline_mode=pl.Buffered(1))


def _params(semantics):
    return pltpu.CompilerParams(dimension_semantics=semantics, vmem_limit_bytes=VMEM_LIMIT_BYTES)


def _cast_specs(weights, n_steps):
    in_specs, out_specs, out_shapes = [], [], []
    for w in weights:
        rows, cols = w.shape
        block = (rows // n_steps, cols)
        in_specs.append(pl.BlockSpec(block, lambda t: (t, 0)))
        out_specs.append(pl.BlockSpec(block, lambda t: (t, 0)))
        out_shapes.append(jax.ShapeDtypeStruct(w.shape, _BF16))
    return in_specs, out_specs, out_shapes


def _cast_blocks(src_refs, dst_refs):
    for src, dst in zip(src_refs, dst_refs):
        dst[...] = src[...].astype(_BF16)


def _conv_mixer_kernel(group_tiles, seq_tiles, n_cast, *refs):
    n_groups = len(group_tiles)
    x_refs = [refs[3 * gi:3 * gi + 3] for gi in range(n_groups)]
    g_ref, w_in_ref, cw_ref, cb_ref, w_out_ref = refs[3 * n_groups:3 * n_groups + 5]
    rest = refs[3 * n_groups + 5:]
    o_ref, u_ref = rest[n_cast], rest[-1]
    _cast_blocks(rest[:n_cast], rest[n_cast + 1:-1])

    def mix(x_ref, xp_ref, xn_ref, i, n):
        x = x_ref[0]
        tm = x.shape[0]
        ext = tm + 2 * HALO_ROWS
        lo, hi = HALO_ROWS, HALO_ROWS + tm
        y_ext = _rms_norm(jnp.concatenate([xp_ref[0], x, xn_ref[0]], axis=0), g_ref[...])
        xe = y_ext.astype(_BF16)

        d = D_MODEL
        u_ext = _dot(xe, w_in_ref[:, 0:d]) * _dot(xe, w_in_ref[:, 2 * d:3 * d])
        u_ref[0:lo, :] = jnp.where(i > 0, u_ext[0:lo], 0.0)
        u_ref[lo:hi, :] = u_ext[lo:hi]
        u_ref[hi:ext, :] = jnp.where(i < n - 1, u_ext[hi:ext], 0.0)
        u_prev = u_ref[lo - 1:hi - 1, :]
        u_next = u_ref[lo + 1:hi + 1, :]
        z = cb_ref[...] + u_prev * cw_ref[0:1, :] + u_ext[lo:hi] * cw_ref[1:2, :] + u_next * cw_ref[2:3, :]

        gate_b = _dot(xe, w_in_ref[:, d:2 * d])[lo:hi]
        y = _dot((gate_b * z).astype(_BF16), w_out_ref[...])
        o_ref[0] = x_ref[0] + y

    t = pl.program_id(0)
    start = 0
    for (x_ref, xp_ref, xn_ref), n, per_seq in zip(x_refs, group_tiles, seq_tiles):
        @pl.when(jnp.logical_and(t >= start, t < start + n))
        def _(x_ref=x_ref, xp_ref=xp_ref, xn_ref=xn_ref, start=start, per_seq=per_seq):
            mix(x_ref, xp_ref, xn_ref, (t - start) % per_seq, per_seq)
        start += n


def _conv_mixer(groups, g, w_in, conv_w, conv_b, w_out, to_cast=()):
    d = groups[0].shape[-1]
    tm = MLP_ROW_TILE
    hb = tm // HALO_ROWS
    group_tiles = [x.shape[0] * x.shape[1] // tm for x in groups]
    seq_tiles = [x.shape[1] // tm for x in groups]
    n_tiles = sum(group_tiles)

    def group_specs(start, n, per_seq):
        last_halo = per_seq * hb - 1

        def where(t):
            local = jnp.clip(t - start, 0, n - 1)
            return local // per_seq, local % per_seq

        def main(t):
            bi, i = where(t)
            return bi, i, 0

        def before(t):
            bi, i = where(t)
            return bi, jnp.maximum(i * hb - 1, 0), 0

        def after(t):
            bi, i = where(t)
            return bi, jnp.minimum((i + 1) * hb, last_halo), 0

        return [pl.BlockSpec((1, tm, d), main), pl.BlockSpec((1, HALO_ROWS, d), before),
                pl.BlockSpec((1, HALO_ROWS, d), after)]

    x_specs, start = [], 0
    for n, per_seq in zip(group_tiles, seq_tiles):
        x_specs += group_specs(start, n, per_seq)
        start += n
    cast_in, cast_out, cast_shapes = _cast_specs(to_cast, n_tiles)
    return pl.pallas_call(
        functools.partial(_conv_mixer_kernel, tuple(group_tiles), tuple(seq_tiles), len(to_cast)),
        grid=(n_tiles,),
        in_specs=x_specs + [_resident((1, d)), _resident((d, 3 * d)), _resident((CONV_WIDTH, d)),
                            _resident((1, d)), _resident((d, d))] + cast_in,
        out_specs=[pl.BlockSpec((1, tm, d), lambda t: (t, 0, 0))] + cast_out,
        out_shape=[jax.ShapeDtypeStruct((n_tiles, tm, d), _F32)] + cast_shapes,
        scratch_shapes=[pltpu.VMEM((tm + 2 * HALO_ROWS, d), _F32)],
        compiler_params=_params(("arbitrary",)),
        name="conv_mixer",
    )(*[x for x in groups for _ in range(3)], g, w_in, conv_w, conv_b, w_out, *to_cast)


def _mlp_body(x, g_ref, w_up_ref, w_down_ref):
    xn = _rms_norm(x, g_ref[...]).astype(_BF16)
    hidden = []
    for c in range(D_FF // FF_CHUNK):
        cols = slice(c * FF_CHUNK, (c + 1) * FF_CHUNK)
        hidden.append(jnp.square(jnp.maximum(_dot(xn, w_up_ref[:, cols]), 0.0)).astype(_BF16))
    return _dot(jnp.concatenate(hidden, axis=1), w_down_ref[...])


def _mlp_kernel(n_cast, x_ref, g_ref, w_up_ref, w_down_ref, *rest):
    o_ref = rest[n_cast]
    _cast_blocks(rest[:n_cast], rest[n_cast + 1:])
    o_ref[...] = x_ref[...] + _mlp_body(x_ref[...], g_ref, w_up_ref, w_down_ref)


def _proj_mlp_norm_kernel(group_tiles, x_ref, r_ref, sg_ref, w_o_ref, g_ref, w_up_ref, w_down_ref, gf_ref, *o_refs):
    def tile_result():
        proj = None
        for h in range(RET_HEADS):
            cols = slice(h * RET_V_DIM, (h + 1) * RET_V_DIM)
            o = r_ref[:, cols].astype(_F32)
            o = o - jnp.mean(o, axis=-1, keepdims=True)
            o = o * lax.rsqrt(jnp.mean(o * o, axis=-1, keepdims=True) + NORM_EPS)
            gated = (sg_ref[:, cols].astype(_F32) * o).astype(_BF16)
            part = _dot(gated, w_o_ref[cols, :])
            proj = part if proj is None else proj + part
        x = x_ref[...] + proj
        return _rms_norm(x + _mlp_body(x, g_ref, w_up_ref, w_down_ref), gf_ref[...])

    t = pl.program_id(0)
    start = 0
    for n, o_ref in zip(group_tiles, o_refs):
        @pl.when(jnp.logical_and(t >= start, t < start + n))
        def _(o_ref=o_ref):
            o_ref[...] = tile_result()
        start += n


def _mlp(x, g, w_up, w_down, to_cast=()):
    t, d = x.shape
    tm = MLP_ROW_TILE
    rows = pl.BlockSpec((tm, d), lambda i: (i, 0))
    cast_in, cast_out, cast_shapes = _cast_specs(to_cast, t // tm)
    return pl.pallas_call(
        functools.partial(_mlp_kernel, len(to_cast)),
        grid=(t // tm,),
        in_specs=[rows, _resident((1, d)), _resident((d, D_FF)), _resident((D_FF, d))] + cast_in,
        out_specs=[rows] + cast_out,
        out_shape=[jax.ShapeDtypeStruct((t, d), _F32)] + cast_shapes,
        compiler_params=_params(("parallel",)),
        name="mlp0",
    )(x, g, w_up, w_down, *to_cast)


def _proj_mlp_norm(stream, r, sg, group_rows, w_o, g, w_up, w_down, g_final):
    t, d = stream.shape
    tm = ROW_TILE
    group_tiles = tuple(n // tm for n in group_rows)
    rows = pl.BlockSpec((tm, d), lambda i: (i, 0))
    wide = pl.BlockSpec((tm, RET_V_WIDTH), lambda i: (i, 0))

    def group_out(start, n):
        return pl.BlockSpec((tm, d), lambda i: (jnp.clip(i - start, 0, n - 1), 0))

    starts = [sum(group_tiles[:gi]) for gi in range(len(group_tiles))]
    return pl.pallas_call(
        functools.partial(_proj_mlp_norm_kernel, group_tiles),
        grid=(t // tm,),
        in_specs=[rows, wide, wide, _resident((RET_V_WIDTH, d)),
                  _resident((1, d)), _resident((d, D_FF)), _resident((D_FF, d)), _resident((1, d))],
        out_specs=[group_out(s0, n) for s0, n in zip(starts, group_tiles)],
        out_shape=[jax.ShapeDtypeStruct((n, d), _F32) for n in group_rows],
        compiler_params=_params(("arbitrary",)),
        name="proj_mlp1_norm",
    )(stream, r, sg, w_o, g, w_up, w_down, g_final)


def _rotary(t, cos, sin):
    half = RET_QK_DIM // 2
    parts = []
    for h in range(RET_HEADS):
        x1 = t[:, h * RET_QK_DIM:h * RET_QK_DIM + half]
        x2 = t[:, h * RET_QK_DIM + half:(h + 1) * RET_QK_DIM]
        parts += [x1 * cos - x2 * sin, x1 * sin + x2 * cos]
    return jnp.concatenate(parts, axis=-1)


def _qkvg_kernel(x_ref, g_ref, w_ref, cos_ref, sin_ref, q_ref, k_ref, v_ref, sg_ref):
    xn = _rms_norm(x_ref[0], g_ref[...]).astype(_BF16)
    qw, vw = RET_QK_WIDTH, RET_V_WIDTH
    gate = _dot(xn, w_ref[:, 2 * qw + vw:2 * qw + 2 * vw])
    half_gate = 0.5 * gate
    sg_ref[0] = (half_gate * (jnp.tanh(half_gate) + 1.0)).astype(_BF16)
    q = _rotary(_dot(xn, w_ref[:, 0:qw]), cos_ref[...], sin_ref[...]) * (RET_QK_DIM ** -0.5)
    q_ref[0] = q.astype(_BF16)
    k_ref[0] = _rotary(_dot(xn, w_ref[:, qw:2 * qw]), cos_ref[...], sin_ref[...]).astype(_BF16)
    v_ref[0] = _dot(xn, w_ref[:, 2 * qw:2 * qw + vw]).astype(_BF16)


def _position_tile(t, group_tiles, seq_tiles):
    pos, start = None, 0
    for n, per_seq in zip(group_tiles, seq_tiles):
        here = (t - start) % per_seq
        pos = here if pos is None else jnp.where(t >= start, here, pos)
        start += n
    return pos


def _qkvg(stream, group_tiles, seq_tiles, g, w, cos, sin):
    n_tiles, tm, d = stream.shape
    half = RET_QK_DIM // 2

    def rows(width):
        return pl.BlockSpec((1, tm, width), lambda t: (t, 0, 0))

    def out(width):
        return jax.ShapeDtypeStruct((n_tiles, tm, width), _BF16)

    table = pl.BlockSpec((tm, half), lambda t: (_position_tile(t, group_tiles, seq_tiles), 0))
    return pl.pallas_call(
        _qkvg_kernel,
        grid=(n_tiles,),
        in_specs=[rows(d), _resident((1, d)), _resident((d, 2 * RET_QK_WIDTH + 2 * RET_V_WIDTH)), table, table],
        out_specs=[rows(RET_QK_WIDTH), rows(RET_QK_WIDTH), rows(RET_V_WIDTH), rows(RET_V_WIDTH)],
        out_shape=[out(RET_QK_WIDTH), out(RET_QK_WIDTH), out(RET_V_WIDTH), out(RET_V_WIDTH)],
        compiler_params=_params(("parallel",)),
        name="qkvg_rotary",
    )(stream, g, w, cos, sin)


def _retention_kernel(group_blocks, group_seq, q_ref, k_ref, v_ref, dmat_ref, dqf_ref, dkf_ref, cdf_ref,
                      dqb_ref, dkb_ref, cdb_ref, o_ref):
    c = RET_CHUNK
    block_rows = q_ref.shape[1]
    contract_rows = (((0,), (0,)), ((), ()))
    contract_cols = (((1,), (1,)), ((), ()))

    def decayed_kv(rows, dk_ref):
        kd = (k_ref[0, rows, :].astype(_F32) * dk_ref[0]).astype(_BF16)
        return lax.dot_general(kd, v_ref[0, rows, :], contract_rows, preferred_element_type=_F32)

    def emit(rows, earlier, later):
        qc = q_ref[0, rows, :]
        scores = lax.dot_general(qc, k_ref[0, rows, :], contract_cols, preferred_element_type=_F32) * dmat_ref[0]
        o = _dot(scores.astype(_BF16), v_ref[0, rows, :])
        qf = qc.astype(_F32)
        scaled, states = [], []
        if earlier is not None:
            scaled.append((qf * dqf_ref[0]).astype(_BF16))
            states.append(earlier)
        if later is not None:
            scaled.append((qf * dqb_ref[0]).astype(_BF16))
            states.append(later)
        if scaled:
            o = o + _dot(jnp.concatenate(scaled, axis=1), jnp.concatenate(states, axis=0))
        o_ref[0, rows, :] = o.astype(o_ref.dtype)

    def sequences(seq_len):
        n_chunks = seq_len // c
        for first_row in range(0, block_rows, seq_len):
            def chunk(j):
                return slice(first_row + j * c, first_row + (j + 1) * c)

            earlier = [None] * n_chunks
            later = [None] * n_chunks
            sf = sb = None
            for t in range(n_chunks - 1):
                kv = decayed_kv(chunk(t), dkf_ref)
                sf = kv if sf is None else sf * cdf_ref[0] + kv
                earlier[t + 1] = sf.astype(_BF16)
                j = n_chunks - 1 - t
                kv = decayed_kv(chunk(j), dkb_ref)
                sb = kv if sb is None else sb * cdb_ref[0] + kv
                later[j - 1] = sb.astype(_BF16)
            for j in range(n_chunks):
                emit(chunk(j), earlier[j], later[j])

    block = pl.program_id(0)
    start = 0
    for n, seq_len in zip(group_blocks, group_seq):
        pl.when(jnp.logical_and(block >= start, block < start + n))(functools.partial(sequences, seq_len))
        start += n


def _retention_tables():
    c = RET_CHUNK
    f32 = np.float32
    h = np.arange(RET_HEADS, dtype=f32)
    lg_f = np.log(f32(1.0) - np.power(f32(2.0), f32(-5.0) - h)).astype(f32)[:, None, None]
    lg_b = np.log(f32(1.0) - np.power(f32(2.0), f32(-5.5) - h)).astype(f32)[:, None, None]
    idx = np.arange(c, dtype=f32)
    diff = idx[:, None] - idx[None, :]
    dmat = np.where(diff >= 0, np.exp(lg_f * np.maximum(diff, f32(0.0))), np.exp(lg_b * np.maximum(-diff, f32(0.0))))
    col = idx[None, :, None]

    def wide(t, width):
        return jnp.asarray(np.ascontiguousarray(np.broadcast_to(t.astype(f32), (RET_HEADS, t.shape[1], width))))

    dqf = wide(np.exp(lg_f * (col + f32(1.0))), RET_QK_DIM)
    dkf = wide(np.exp(lg_f * (f32(c - 1.0) - col)), RET_QK_DIM)
    cdf = wide(np.exp(lg_f * f32(c)), RET_V_DIM)
    dqb = wide(np.exp(lg_b * (f32(c) - col)), RET_QK_DIM)
    dkb = wide(np.exp(lg_b * col), RET_QK_DIM)
    cdb = wide(np.exp(lg_b * f32(c)), RET_V_DIM)
    return jnp.asarray(dmat.astype(f32)), dqf, dkf, cdf, dqb, dkb, cdb


def _retention(q, k, v, tables, group_rows, group_seq):
    c = RET_CHUNK
    rows = q.shape[0]
    step = RET_STEP_ROWS
    assert all(n % step == 0 and step % s == 0 for n, s in zip(group_rows, group_seq))

    def blocks(width):
        return pl.BlockSpec((1, step, width), lambda bi, hi: (bi, 0, hi))

    def per_head(n, width):
        return pl.BlockSpec((1, n, width), lambda bi, hi: (hi, 0, 0))

    def blocked(t):
        return t.reshape(-1, step, t.shape[-1])

    r = pl.pallas_call(
        functools.partial(_retention_kernel, tuple(n // step for n in group_rows), tuple(group_seq)),
        grid=(rows // step, RET_HEADS),
        in_specs=[blocks(RET_QK_DIM), blocks(RET_QK_DIM), blocks(RET_V_DIM),
                  per_head(c, c), per_head(c, RET_QK_DIM), per_head(c, RET_QK_DIM), per_head(1, RET_V_DIM),
                  per_head(c, RET_QK_DIM), per_head(c, RET_QK_DIM), per_head(1, RET_V_DIM)],
        out_specs=blocks(RET_V_DIM),
        out_shape=jax.ShapeDtypeStruct((rows // step, step, RET_V_WIDTH), _BF16),
        compiler_params=_params(("parallel", "parallel")),
        name="retention",
    )(blocked(q), blocked(k), blocked(v), *tables)
    return r.reshape(rows, RET_V_WIDTH)


def _rotary_tables(s):
    half = RET_QK_DIM // 2
    f32 = np.float32
    inv = np.power(f32(ROPE_BASE), -np.arange(half, dtype=f32) / f32(half)).astype(f32)
    ang = (np.arange(s, dtype=f32)[:, None] * inv[None, :]).astype(f32)
    return jnp.asarray(np.cos(ang).astype(f32)), jnp.asarray(np.sin(ang).astype(f32))


CAST_IN_CONV = ("w_up_0", "w_down_0", "w_qkvg_1")
CAST_IN_MLP0 = ("w_o_1", "w_up_1", "w_down_1")


def _trunks(groups, p):
    p = dict(p)
    d = groups[0].shape[-1]
    tm = MLP_ROW_TILE
    tiles = [b * s // tm for b, s, _ in (x.shape for x in groups)]
    stream, *cast = _conv_mixer(groups, p["norm_mix_0"], p["w_in_conv_0"], p["conv_w_0"], p["conv_b_0"],
                                p["w_out_conv_0"], [p[n] for n in CAST_IN_CONV])
    p.update(zip(CAST_IN_CONV, cast))
    stream, *cast = _mlp(stream.reshape(-1, d), p["norm_mlp_0"], p["w_up_0"], p["w_down_0"],
                         [p[n] for n in CAST_IN_MLP0])
    p.update(zip(CAST_IN_MLP0, cast))

    cos, sin = _rotary_tables(max(x.shape[1] for x in groups))
    q, k, v, sg = (t.reshape(-1, t.shape[-1]) for t in
                   _qkvg(stream.reshape(-1, tm, d), tiles, [x.shape[1] // tm for x in groups],
                         p["norm_mix_1"], p["w_qkvg_1"], cos, sin))
    r = _retention(q, k, v, _retention_tables(), [n * tm for n in tiles], [x.shape[1] for x in groups])
    ys = _proj_mlp_norm(stream, r, sg, [n * tm for n in tiles], p["w_o_1"], p["norm_mlp_1"], p["w_up_1"],
                        p["w_down_1"], p["norm_final"])
    return tuple(y.reshape(x.shape) for y, x in zip(ys, groups))


def kernel(x_prompt, x_sample, norm_mix_0, w_in_conv_0, conv_w_0, conv_b_0, w_out_conv_0, norm_mlp_0, w_up_0, w_down_0,
           norm_mix_1, w_qkvg_1, w_o_1, norm_mlp_1, w_up_1, w_down_1, norm_final):
    row = lambda t: t.reshape(1, -1)
    p = {
        "norm_mix_0": row(norm_mix_0), "w_in_conv_0": w_in_conv_0.astype(_BF16), "conv_w_0": conv_w_0,
        "conv_b_0": row(conv_b_0), "w_out_conv_0": w_out_conv_0.astype(_BF16),
        "norm_mlp_0": row(norm_mlp_0), "w_up_0": w_up_0, "w_down_0": w_down_0,
        "norm_mix_1": row(norm_mix_1), "w_qkvg_1": w_qkvg_1, "w_o_1": w_o_1,
        "norm_mlp_1": row(norm_mlp_1), "w_up_1": w_up_1, "w_down_1": w_down_1,
        "norm_final": row(norm_final),
    }
    return _trunks((x_prompt, x_sample), p)
```
